```python
import math
import jax, jax.numpy as jnp
from jax import lax
import numpy as np

D_MODEL = 1024
BATCH = 4
SEQ = 4096
DEPTH = 1

EPS = 1e-6
RET_HEADS = 4
RET_DK = 64
RET_DV = 128
RET_CHUNK = 128
ROPE_BASE = 10000.0
SWA_HEADS = 8
SWA_KV_HEADS = 2
SWA_DH = 64
WINDOW = 128
REL_BUCKETS = 32
REL_MAX_DIST = 128
PEER_HEADS = 8
PEER_NKEYS = 128
PEER_N_EXPERTS = PEER_NKEYS * PEER_NKEYS
PEER_DQ = 256
PEER_TOPK = 16
PEER_TOKEN_BLOCK = 128

RET_Q = RET_HEADS * RET_DK
RET_V = RET_HEADS * RET_DV
SWA_Q = SWA_HEADS * SWA_DH
SWA_KV = SWA_KV_HEADS * SWA_DH
D_IN = 2 * RET_Q + 2 * RET_V + SWA_Q + 2 * SWA_KV
D_MIX = RET_V + SWA_Q

kernel_name = "hybrid_retention_swa_peer_block"


def rms_norm(x, g):
    xf = x.astype(jnp.float32)
    y = xf * lax.rsqrt(jnp.mean(xf * xf, axis=-1, keepdims=True) + EPS)
    return (y * g.astype(jnp.float32)).astype(x.dtype)


def rotary(x, pos):
    d = x.shape[-1]
    inv = 1.0 / (ROPE_BASE ** (np.arange(0, d, 2, dtype=np.float32) / d))
    ang = pos.astype(jnp.float32)[:, None] * jnp.asarray(inv)[None, :]
    cos = jnp.cos(ang)[None, :, None, :]
    sin = jnp.sin(ang)[None, :, None, :]
    xf = x.astype(jnp.float32)
    x1, x2 = xf[..., : d // 2], xf[..., d // 2:]
    return jnp.concatenate([x1 * cos - x2 * sin, x1 * sin + x2 * cos], axis=-1).astype(x.dtype)


def retention(q, k, v):
    B, S, H, dk = q.shape
    dv = v.shape[-1]
    C = RET_CHUNK
    n = S // C
    f32 = jnp.float32
    log_gamma = jnp.log(1.0 - 2.0 ** (-5.0 - jnp.arange(H, dtype=f32)))
    qc = q.astype(f32).reshape(B, n, C, H, dk)
    kc = k.astype(f32).reshape(B, n, C, H, dk)
    vc = v.astype(f32).reshape(B, n, C, H, dv)
    idx = jnp.arange(C, dtype=f32)
    diff = idx[:, None] - idx[None, :]
    decay_mask = jnp.where(diff[None] >= 0,
                           jnp.exp(jnp.maximum(diff, 0.0)[None] * log_gamma[:, None, None]),
                           0.0)
    scores = jnp.einsum('bnihd,bnjhd->bnhij', qc, kc) * decay_mask[None, None]
    o_inner = jnp.einsum('bnhij,bnjhe->bnihe', scores, vc)
    zeta = jnp.exp((C - 1.0 - idx)[None, :] * log_gamma[:, None])
    contrib = jnp.einsum('bnjhd,hj,bnjhe->nbhde', kc, zeta, vc)
    chunk_decay = jnp.exp(C * log_gamma)[None, :, None, None]

    def step(state, c):
        return state * chunk_decay + c, state

    _, prev = lax.scan(step, jnp.zeros((B, H, dk, dv), f32), contrib)
    xi = jnp.exp((idx + 1.0)[None, :] * log_gamma[:, None])
    o_cross = jnp.einsum('bnihd,nbhde,hi->bnihe', qc, prev, xi)
    return (o_inner + o_cross).reshape(B, S, H, dv)


def t5_bucket(rel):
    max_exact = REL_BUCKETS // 2
    n = np.maximum(rel, 0)
    large = max_exact + (np.log(np.maximum(n, 1).astype(np.float32) / max_exact)
                         / math.log(REL_MAX_DIST / max_exact)
                         * (REL_BUCKETS - max_exact)).astype(np.int32)
    large = np.minimum(large, REL_BUCKETS - 1)
    return np.where(n < max_exact, n, large).astype(np.int32)


def sliding_window_gqa(q, k, v, sinks, rel_bias):
    B, S, Hq, d = q.shape
    Hkv = k.shape[2]
    G = Hq // Hkv
    W = WINDOW
    nb = S // W
    f32 = jnp.float32
    qb = q.reshape(B, nb, W, Hkv, G, d)
    pad = jnp.zeros((B, W, Hkv, d), k.dtype)
    kp = jnp.concatenate([pad, k], axis=1).reshape(B, nb + 1, W, Hkv, d)
    vp = jnp.concatenate([pad, v], axis=1).reshape(B, nb + 1, W, Hkv, d)
    kw = jnp.concatenate([kp[:, :-1], kp[:, 1:]], axis=2)
    vw = jnp.concatenate([vp[:, :-1], vp[:, 1:]], axis=2)
    s = jnp.einsum('bnqhgd,bnkhd->bnhgqk', qb, kw, preferred_element_type=f32) * (d ** -0.5)
    rel = np.arange(W)[:, None] + W - np.arange(2 * W)[None, :]
    band = (rel >= 0) & (rel < W)
    bias = rel_bias[t5_bucket(rel)].astype(f32)
    bias = jnp.transpose(bias, (2, 0, 1)).reshape(Hkv, G, W, 2 * W)
    key_pos = np.arange(nb)[:, None] * W - W + np.arange(2 * W)[None, :]
    valid = band[None] & (key_pos >= 0)[:, None, :]
    s = s + bias[None, None]
    s = jnp.where(valid[None, :, None, None], s, -jnp.inf)
    sink = sinks.astype(f32).reshape(Hkv, G)[None, None, :, :, None, None]
    m = jnp.maximum(jnp.max(s, axis=-1, keepdims=True), sink)
    p = jnp.exp(s - m)
    p = p / (jnp.sum(p, axis=-1, keepdims=True) + jnp.exp(sink - m))
    o = jnp.einsum('bnhgqk,bnkhd->bnqhgd', p, vw.astype(f32))
    return o.reshape(B, S, Hq * d)


def peer(x, wq, subkeys, u_tab, v_tab):
    B, S, D = x.shape
    T = B * S
    H, K = PEER_HEADS, PEER_TOPK
    f32 = jnp.float32
    xt = x.reshape(T, D)
    q = jnp.matmul(xt, wq).reshape(T, H, 2, PEER_DQ // 2)
    scores = jnp.einsum('thpc,hpkc->thpk', q, subkeys, preferred_element_type=f32)
    half_s, half_i = lax.top_k(scores, K)
    cand_s = (half_s[:, :, 0, :, None] + half_s[:, :, 1, None, :]).reshape(T, H, K * K)
    cand_i = (half_i[:, :, 0, :, None] * PEER_NKEYS + half_i[:, :, 1, None, :]).reshape(T, H, K * K)
    top_s, pos = lax.top_k(cand_s, K)
    eidx = jnp.take_along_axis(cand_i, pos, axis=-1)
    gates = jax.nn.softmax(top_s, axis=-1)
    Tb = PEER_TOKEN_BLOCK
    nblk = T // Tb

    def block(args):
        xb, ib, gb = args
        ue = u_tab[ib]
        hcur = jnp.einsum('thkd,td->thk', ue, xb, preferred_element_type=f32)
        a = gb * jax.nn.gelu(hcur, approximate=False)
        ve = v_tab[ib]
        return jnp.einsum('thk,thkd->td', a, ve.astype(f32)).astype(x.dtype)

    out = lax.map(block, (xt.reshape(nblk, Tb, D),
                          eidx.reshape(nblk, Tb, H, K),
                          gates.reshape(nblk, Tb, H, K)))
    return out.reshape(B, S, D)


def setup_inputs(seed: int = 0) -> dict:
    key = jax.random.key(seed)
    ks = jax.random.split(key, 16)
    f32 = jnp.float32
    L, D = DEPTH, D_MODEL
    nrm = lambda k, shape, std: jax.random.normal(k, shape, f32) * std
    return {
        "x": nrm(ks[0], (BATCH, SEQ, D), 1.0),
        "norm1_g": 1.0 + nrm(ks[1], (L, D), 0.02),
        "w_in": nrm(ks[2], (L, D, D_IN), D ** -0.5),
        "ret_gn_g": 1.0 + nrm(ks[3], (L, RET_V), 0.02),
        "swa_sinks": nrm(ks[4], (L, SWA_HEADS), 0.5),
        "rel_bias": nrm(ks[5], (REL_BUCKETS, SWA_HEADS), 0.5),
        "w_out": nrm(ks[6], (L, D_MIX, D), D_MIX ** -0.5),
        "norm2_g": 1.0 + nrm(ks[7], (L, D), 0.02),
        "peer_wq": nrm(ks[8], (L, D, PEER_HEADS * PEER_DQ), D ** -0.5),
        "peer_subkeys": nrm(ks[9], (L, PEER_HEADS, 2, PEER_NKEYS, PEER_DQ // 2), (PEER_DQ // 2) ** -0.5),
        "peer_u": nrm(ks[10], (L, PEER_N_EXPERTS, D), D ** -0.5),
        "peer_v": nrm(ks[11], (L, PEER_N_EXPERTS, D), 0.1),
        "final_g": 1.0 + nrm(ks[12], (D,), 0.02),
    }


def reference(x, norm1_g, w_in, ret_gn_g, swa_sinks, rel_bias, w_out, norm2_g,
              peer_wq, peer_subkeys, peer_u, peer_v, final_g):
    B, S, D = x.shape
    pos = jnp.arange(S)
    split_at = list(np.cumsum([RET_Q, RET_Q, RET_V, RET_V, SWA_Q, SWA_KV]))
    h = x
    for l in range(DEPTH):
        y = rms_norm(h, norm1_g[l])
        proj = jnp.matmul(y, w_in[l])
        r_q, r_k, r_v, r_g, s_q, s_k, s_v = jnp.split(proj, split_at, axis=-1)
        rq = rotary(r_q.reshape(B, S, RET_HEADS, RET_DK), pos)
        rk = rotary(r_k.reshape(B, S, RET_HEADS, RET_DK), pos) * (RET_DK ** -0.5)
        rv = r_v.reshape(B, S, RET_HEADS, RET_DV)
        ro = retention(rq, rk, rv)
        mu = jnp.mean(ro, axis=-1, keepdims=True)
        var = jnp.mean(jnp.square(ro - mu), axis=-1, keepdims=True)
        ro = ((ro - mu) * lax.rsqrt(var + EPS)).reshape(B, S, RET_V) * ret_gn_g[l].astype(jnp.float32)
        ret_out = (jax.nn.silu(r_g.astype(jnp.float32)) * ro).astype(h.dtype)
        swa_out = sliding_window_gqa(s_q.reshape(B, S, SWA_HEADS, SWA_DH),
                                     s_k.reshape(B, S, SWA_KV_HEADS, SWA_DH),
                                     s_v.reshape(B, S, SWA_KV_HEADS, SWA_DH),
                                     swa_sinks[l], rel_bias).astype(h.dtype)
        mixed = jnp.concatenate([ret_out, swa_out], axis=-1)
        h = h + jnp.matmul(mixed, w_out[l])
        h = h + peer(rms_norm(h, norm2_g[l]), peer_wq[l], peer_subkeys[l], peer_u[l], peer_v[l])
    return rms_norm(h, final_g)
```

```python
import functools
import math

import numpy as np
import jax
import jax.numpy as jnp
from jax import lax
from jax.experimental import pallas as pl
from jax.experimental.pallas import tpu as pltpu

F32 = jnp.float32
BF16 = jnp.bfloat16

EPS = 1e-6
RET_HEADS = 4
RET_DK = 64
RET_DV = 128
RET_CHUNK = 128
ROPE_BASE = 10000.0
SWA_HEADS = 8
SWA_KV_HEADS = 2
SWA_DH = 64
WINDOW = 128
REL_BUCKETS = 32
REL_MAX_DIST = 128
PEER_HEADS = 8
PEER_NKEYS = 128
PEER_DQ = 256
PEER_TOPK = 16

RET_Q = RET_HEADS * RET_DK
RET_V = RET_HEADS * RET_DV
SWA_Q = SWA_HEADS * SWA_DH
SWA_KV = SWA_KV_HEADS * SWA_DH
SWA_GROUP = SWA_HEADS // SWA_KV_HEADS

LANES = 128
VMEM_LIMIT = 48 * 1024 * 1024
NEG = -1e30

_STAIR_COUNTS = [PEER_TOPK // (k + 1) for k in range(PEER_TOPK)]
_STAIR_OFFS = [int(v) for v in np.cumsum([0] + _STAIR_COUNTS[:-1])]
_STAIR_ROWS = int(sum(_STAIR_COUNTS))
_STAIR_PAD = -(-_STAIR_ROWS // 8) * 8


def _dot(a, b):
    return jnp.dot(a, b, preferred_element_type=F32)


def _dot_nt(a, b):
    return lax.dot_general(a, b, (((1,), (1,)), ((), ())), preferred_element_type=F32)


def _dot_tn(a, b):
    return lax.dot_general(a, b, (((0,), (0,)), ((), ())), preferred_element_type=F32)


def _params(*sem):
    return pltpu.CompilerParams(dimension_semantics=sem, vmem_limit_bytes=VMEM_LIMIT)


def _inproj_kernel(x_ref, g_ref, w_ref, cos_ref, sin_ref,
                   rq_ref, rk_ref, rv_ref, rg_ref, sq_ref, sk_ref, svv_ref):
    x = x_ref[...]
    y = x * lax.rsqrt(jnp.mean(x * x, axis=-1, keepdims=True) + EPS) * g_ref[...]
    proj = _dot(y.astype(BF16), w_ref[...])
    cos = cos_ref[...]
    sin = sin_ref[...]
    lane = lax.broadcasted_iota(jnp.int32, cos.shape, 1)
    first_half = (lane % RET_DK) < (RET_DK // 2)

    def rot(v):
        swapped = jnp.where(first_half, pltpu.roll(v, RET_Q - RET_DK // 2, 1),
                            pltpu.roll(v, RET_DK // 2, 1))
        return v * cos + swapped * sin

    c = 0
    rq = rot(proj[:, c:c + RET_Q]); c += RET_Q
    rk = rot(proj[:, c:c + RET_Q]) * (RET_DK ** -0.5); c += RET_Q
    for h in range(RET_HEADS):
        rq_ref[h] = rq[:, h * RET_DK:(h + 1) * RET_DK]
        rk_ref[h] = rk[:, h * RET_DK:(h + 1) * RET_DK]
    rv_ref[...] = proj[:, c:c + RET_V].astype(BF16); c += RET_V
    rg_ref[...] = proj[:, c:c + RET_V]; c += RET_V
    for h in range(SWA_HEADS):
        sq_ref[h] = proj[:, c + h * SWA_DH:c + (h + 1) * SWA_DH].astype(BF16)
    c += SWA_Q
    for h in range(SWA_KV_HEADS):
        sk_ref[h] = proj[:, c + h * SWA_DH:c + (h + 1) * SWA_DH].astype(BF16)
    c += SWA_KV
    svv_ref[...] = proj[:, c:c + 2 * SWA_KV].astype(BF16)


def _inproj(x2, g1, w1, cos_t, sin_t, S, tm):
    T, D = x2.shape
    n_s = S // tm
    ncol = w1.shape[1]
    tok = lambda i: (i, 0)
    hm = lambda i: (0, i, 0)
    return pl.pallas_call(
        _inproj_kernel,
        grid=(T // tm,),
        in_specs=[
            pl.BlockSpec((tm, D), tok),
            pl.BlockSpec((1, D), lambda i: (0, 0)),
            pl.BlockSpec((D, ncol), lambda i: (0, 0)),
            pl.BlockSpec((tm, RET_Q), lambda i: (i % n_s, 0)),
            pl.BlockSpec((tm, RET_Q), lambda i: (i % n_s, 0)),
        ],
        out_specs=[
            pl.BlockSpec((RET_HEADS, tm, RET_DK), hm),
            pl.BlockSpec((RET_HEADS, tm, RET_DK), hm),
            pl.BlockSpec((tm, RET_V), tok),
            pl.BlockSpec((tm, RET_V), tok),
            pl.BlockSpec((SWA_HEADS, tm, SWA_DH), hm),
            pl.BlockSpec((SWA_KV_HEADS, tm, SWA_DH), hm),
            pl.BlockSpec((tm, 2 * SWA_KV), tok),
        ],
        out_shape=[
            jax.ShapeDtypeStruct((RET_HEADS, T, RET_DK), F32),
            jax.ShapeDtypeStruct((RET_HEADS, T, RET_DK), F32),
            jax.ShapeDtypeStruct((T, RET_V), BF16),
            jax.ShapeDtypeStruct((T, RET_V), F32),
            jax.ShapeDtypeStruct((SWA_HEADS, T, SWA_DH), BF16),
            jax.ShapeDtypeStruct((SWA_KV_HEADS, T, SWA_DH), BF16),
            jax.ShapeDtypeStruct((T, 2 * SWA_KV), BF16),
        ],
        compiler_params=_params("parallel"),
        name="inproj",
    )(x2, g1, w1, cos_t, sin_t)


def _retention_kernel(q_ref, k_ref, v_ref, g_ref, dm_ref, xi_ref, zeta_ref, cd_ref, gn_ref,
                      o_ref, state_ref, *, n_chunks):
    @pl.when(pl.program_id(2) == 0)
    def _():
        state_ref[...] = jnp.zeros_like(state_ref)

    dm = dm_ref[0]
    xi = xi_ref[0]
    zeta = zeta_ref[0]
    cd = cd_ref[0]
    gn = gn_ref[...]

    def chunk(c, carry):
        r = pl.multiple_of(c * RET_CHUNK, RET_CHUNK)
        q = q_ref[0, pl.ds(r, RET_CHUNK), :]
        k = k_ref[0, pl.ds(r, RET_CHUNK), :]
        v = v_ref[pl.ds(r, RET_CHUNK), :]
        g = g_ref[pl.ds(r, RET_CHUNK), :]
        st = state_ref[...]
        s = _dot_nt(q.astype(BF16), k.astype(BF16)) * dm
        o = _dot(s.astype(BF16), v) + _dot((q * xi).astype(BF16), st.astype(BF16))
        state_ref[...] = st * cd + _dot_tn((k * zeta).astype(BF16), v)
        mu = jnp.mean(o, axis=-1, keepdims=True)
        oc = o - mu
        var = jnp.mean(oc * oc, axis=-1, keepdims=True)
        on = oc * lax.rsqrt(var + EPS) * gn
        o_ref[pl.ds(r, RET_CHUNK), :] = (g * jax.nn.sigmoid(g)) * on
        return carry

    lax.fori_loop(0, n_chunks, chunk, 0)


def _retention(rq, rk, rv, rg, gn, B, S, tc):
    T = rv.shape[0]
    n_c = S // tc
    C = RET_CHUNK
    lg = np.log(1.0 - 2.0 ** (-5.0 - np.arange(RET_HEADS, dtype=np.float64)))
    idx = np.arange(C, dtype=np.float64)
    diff = idx[:, None] - idx[None, :]
    dm = np.where(diff[None] >= 0, np.exp(np.maximum(diff, 0.0)[None] * lg[:, None, None]), 0.0)
    xi = np.broadcast_to(np.exp((idx + 1.0)[None, :] * lg[:, None])[:, :, None], (RET_HEADS, C, RET_DK))
    zeta = np.broadcast_to(np.exp((C - 1.0 - idx)[None, :] * lg[:, None])[:, :, None], (RET_HEADS, C, RET_DK))
    cd = np.broadcast_to(np.exp(C * lg)[:, None, None], (RET_HEADS, RET_DK, RET_DV))
    const = lambda a: jnp.asarray(np.ascontiguousarray(a), F32)
    hm = lambda b, h, c: (h, b * n_c + c, 0)
    tokh = lambda b, h, c: (b * n_c + c, h)
    perh = lambda b, h, c: (h, 0, 0)
    return pl.pallas_call(
        functools.partial(_retention_kernel, n_chunks=tc // C),
        grid=(B, RET_HEADS, n_c),
        in_specs=[
            pl.BlockSpec((1, tc, RET_DK), hm),
            pl.BlockSpec((1, tc, RET_DK), hm),
            pl.BlockSpec((tc, RET_DV), tokh),
            pl.BlockSpec((tc, RET_DV), tokh),
            pl.BlockSpec((1, C, C), perh),
            pl.BlockSpec((1, C, RET_DK), perh),
            pl.BlockSpec((1, C, RET_DK), perh),
            pl.BlockSpec((1, RET_DK, RET_DV), perh),
            pl.BlockSpec((1, RET_DV), lambda b, h, c: (0, h)),
        ],
        out_specs=pl.BlockSpec((tc, RET_DV), tokh),
        out_shape=jax.ShapeDtypeStruct((T, RET_V), F32),
        scratch_shapes=[pltpu.VMEM((RET_DK, RET_DV), F32)],
        compiler_params=_params("parallel", "parallel", "arbitrary"),
        name="retention",
    )(rq, rk, rv, rg, const(dm), const(xi), const(zeta), const(cd), gn)


def _t5_bucket(rel):
    max_exact = REL_BUCKETS // 2
    n = np.maximum(rel, 0)
    large = max_exact + (np.log(np.maximum(n, 1).astype(np.float32) / max_exact)
                         / math.log(REL_MAX_DIST / max_exact)
                         * (REL_BUCKETS - max_exact)).astype(np.int32)
    large = np.minimum(large, REL_BUCKETS - 1)
    return np.where(n < max_exact, n, large).astype(np.int32)


def _swa_bias_kernel(bucket_ref, rb_ref, o_ref):
    bucket = bucket_ref[...]
    for h in range(SWA_HEADS):
        acc = jnp.zeros(bucket.shape, F32)
        for b in range(REL_BUCKETS):
            acc = jnp.where(bucket == b, rb_ref[b, h], acc)
        o_ref[h] = acc


def _swa_bias(rel_bias):
    W = WINDOW
    rel = np.arange(W)[:, None] + W - np.arange(2 * W)[None, :]
    bucket = jnp.asarray(_t5_bucket(rel), jnp.int32)
    return pl.pallas_call(
        _swa_bias_kernel,
        in_specs=[pl.BlockSpec(memory_space=pltpu.VMEM), pl.BlockSpec(memory_space=pltpu.SMEM)],
        out_specs=pl.BlockSpec(memory_space=pltpu.VMEM),
        out_shape=jax.ShapeDtypeStruct((SWA_HEADS, W, 2 * W), F32),
        name="swa_bias",
    )(bucket, rel_bias.astype(F32))


def _swa_kernel(sink_ref, q_ref, kc_ref, kp_ref, vc_ref, vp_ref, bias_ref, o_ref, *, n_sub):
    W = WINDOW
    kvh = pl.program_id(1)
    first_step = pl.program_id(2) == 0
    qi = lax.broadcasted_iota(jnp.int32, (W, 2 * W), 0)
    kj = lax.broadcasted_iota(jnp.int32, (W, 2 * W), 1)
    rel = qi + W - kj
    band = (rel >= 0) & (rel < W)
    lane = lax.broadcasted_iota(jnp.int32, (2 * W, 2 * SWA_DH), 1)
    lo = lane < SWA_DH
    scale = SWA_DH ** -0.5
    for j in range(n_sub):
        if j == 0:
            kp = kp_ref[0]
            vp = vp_ref[...]
            valid = band & ((kj >= W) | jnp.logical_not(first_step))
        else:
            kp = kc_ref[0, (j - 1) * W:j * W, :]
            vp = vc_ref[(j - 1) * W:j * W, :]
            valid = band
        kw = jnp.concatenate([kp, kc_ref[0, j * W:(j + 1) * W, :]], axis=0)
        vw = jnp.concatenate([vp, vc_ref[j * W:(j + 1) * W, :]], axis=0)
        v_lo = jnp.where(lo, vw, jnp.zeros_like(vw))
        v_hi = jnp.where(lo, jnp.zeros_like(vw), vw)
        for pair in range(SWA_GROUP // 2):
            acc = None
            for sub, v_half in ((0, v_lo), (1, v_hi)):
                g = 2 * pair + sub
                q = q_ref[g, j * W:(j + 1) * W, :]
                s = _dot_nt(q, kw) * scale + bias_ref[g]
                s = jnp.where(valid, s, NEG)
                sink = sink_ref[kvh * SWA_GROUP + g]
                m = jnp.maximum(jnp.max(s, axis=-1, keepdims=True), sink)
                p = jnp.exp(s - m)
                den = jnp.sum(p, axis=-1, keepdims=True) + jnp.exp(sink - m)
                p = p / den
                o = _dot(p.astype(BF16), v_half)
                acc = o if acc is None else acc + o
            o_ref[j * W:(j + 1) * W, pair * 2 * SWA_DH:(pair + 1) * 2 * SWA_DH] = acc


def _swa(sq, sk, svv, bias, sinks, B, S, tq):
    T = svv.shape[0]
    W = WINDOW
    n_q = S // tq
    sub = tq // W
    cur3 = lambda b, h, i: (h, b * n_q + i, 0)
    prev3 = lambda b, h, i: (h, jnp.maximum(b * (S // W) + i * sub - 1, b * (S // W)), 0)
    cur2 = lambda b, h, i: (b * n_q + i, h)
    prev2 = lambda b, h, i: (jnp.maximum(b * (S // W) + i * sub - 1, b * (S // W)), h)
    return pl.pallas_call(
        functools.partial(_swa_kernel, n_sub=sub),
        grid=(B, SWA_KV_HEADS, n_q),
        in_specs=[
            pl.BlockSpec(memory_space=pltpu.SMEM),
            pl.BlockSpec((SWA_GROUP, tq, SWA_DH), cur3),
            pl.BlockSpec((1, tq, SWA_DH), cur3),
            pl.BlockSpec((1, W, SWA_DH), prev3),
            pl.BlockSpec((tq, 2 * SWA_DH), cur2),
            pl.BlockSpec((W, 2 * SWA_DH), prev2),
            pl.BlockSpec((SWA_GROUP, W, 2 * W), lambda b, h, i: (h, 0, 0)),
        ],
        out_specs=pl.BlockSpec((tq, SWA_GROUP * SWA_DH), cur2),
        out_shape=jax.ShapeDtypeStruct((T, SWA_Q), F32),
        compiler_params=_params("parallel", "parallel", "arbitrary"),
        name="swa",
    )(sinks, sq, sk, sk, svv, svv, bias)


def _outproj_kernel(x_ref, ret_ref, swa_ref, w_ref, g_ref, h_ref, xn_ref):
    h = (x_ref[...]
         + _dot(ret_ref[...].astype(BF16), w_ref[0:RET_V, :])
         + _dot(swa_ref[...].astype(BF16), w_ref[RET_V:RET_V + SWA_Q, :]))
    h_ref[...] = h
    y = h * lax.rsqrt(jnp.mean(h * h, axis=-1, keepdims=True) + EPS) * g_ref[...]
    xn_ref[...] = y.astype(BF16)


def _outproj(x2, ret_out, swa_out, w_out, g2, tm):
    T, D = x2.shape
    tok = lambda i: (i, 0)
    return pl.pallas_call(
        _outproj_kernel,
        grid=(T // tm,),
        in_specs=[
            pl.BlockSpec((tm, D), tok),
            pl.BlockSpec((tm, RET_V), tok),
            pl.BlockSpec((tm, SWA_Q), tok),
            pl.BlockSpec(w_out.shape, lambda i: (0, 0)),
            pl.BlockSpec((1, D), lambda i: (0, 0)),
        ],
        out_specs=[pl.BlockSpec((tm, D), tok), pl.BlockSpec((tm, D), tok)],
        out_shape=[jax.ShapeDtypeStruct((T, D), F32), jax.ShapeDtypeStruct((T, D), BF16)],
        compiler_params=_params("parallel"),
        name="outproj",
    )(x2, ret_out, swa_out, w_out, g2)


def _extract_top(x, with_rank):
    rank = jnp.full(x.shape, float(PEER_NKEYS - 1), F32) if with_rank else None
    vals = []
    for r in range(PEER_TOPK):
        m = jnp.max(x, axis=0, keepdims=True)
        is_max = x == m
        if with_rank:
            rank = jnp.where(is_max, float(r), rank)
        x = jnp.where(is_max, -jnp.inf, x)
        vals.append(m)
    return vals, rank


def _peer_prep_kernel(xn_ref, wq_ref, sk_ref, n_ref, ea_ref, rb_ref, eb_ref,
                      q_scr, am_scr, bm_scr, cand_scr, *, n_lane_tiles):
    half = PEER_DQ // 2
    q_scr[...] = _dot_nt(wq_ref[...], xn_ref[...]).astype(BF16)

    def head(h, carry):
        for lt in range(n_lane_tiles):
            cols = pl.ds(lt * LANES, LANES)
            ra = pl.multiple_of(h * PEER_DQ, PEER_DQ)
            a = _dot(sk_ref[2 * h], q_scr[pl.ds(ra, half), cols])
            b = _dot(sk_ref[2 * h + 1], q_scr[pl.ds(ra + half, half), cols])
            avals, _ = _extract_top(a, False)
            bvals, rb = _extract_top(b, True)
            for r in range(PEER_TOPK):
                am_scr[r:r + 1, :] = avals[r]
                bm_scr[r:r + 1, :] = bvals[r]
            cand_scr[_STAIR_ROWS:_STAIR_PAD, :] = jnp.full((_STAIR_PAD - _STAIR_ROWS, LANES), -jnp.inf, F32)
            for k in range(PEER_TOPK):
                cnt = _STAIR_COUNTS[k]
                cand_scr[_STAIR_OFFS[k]:_STAIR_OFFS[k] + cnt, :] = avals[k] + bm_scr[0:cnt, :]
            cand = cand_scr[...]
            x = cand
            tau = None
            for r in range(PEER_TOPK):
                tau = jnp.max(x, axis=0, keepdims=True)
                x = jnp.where(x == tau, -jnp.inf, x)
            top = avals[0] + bvals[0]
            z = jnp.sum(jnp.where(cand >= tau, jnp.exp(cand - top), 0.0), axis=0, keepdims=True)
            n = jnp.zeros(a.shape, F32)
            for r in range(PEER_TOPK):
                n = n + jnp.where(a + bvals[r] >= tau, 1.0, 0.0)
            n_ref[h, :, cols] = n
            ea_ref[h, :, cols] = jnp.exp(a - avals[0])
            rb_ref[h, :, cols] = rb.astype(BF16)
            eb_ref[h, :, cols] = (jnp.exp(b - bvals[0]) / z).astype(BF16)
        return carry

    lax.fori_loop(0, PEER_HEADS, head, 0)


def _peer_prep(xn, wq_t, sk, tb):
    T, D = xn.shape
    H, K = PEER_HEADS, PEER_NKEYS
    blk = lambda i: (0, 0, i)
    return pl.pallas_call(
        functools.partial(_peer_prep_kernel, n_lane_tiles=tb // LANES),
        grid=(T // tb,),
        in_specs=[
            pl.BlockSpec((tb, D), lambda i: (i, 0)),
            pl.BlockSpec(wq_t.shape, lambda i: (0, 0)),
            pl.BlockSpec(sk.shape, lambda i: (0, 0, 0)),
        ],
        out_specs=[pl.BlockSpec((H, K, tb), blk)] * 4,
        out_shape=[
            jax.ShapeDtypeStruct((H, K, T), F32),
            jax.ShapeDtypeStruct((H, K, T), F32),
            jax.ShapeDtypeStruct((H, K, T), BF16),
            jax.ShapeDtypeStruct((H, K, T), BF16),
        ],
        scratch_shapes=[
            pltpu.VMEM((H * PEER_DQ, tb), BF16),
            pltpu.VMEM((PEER_TOPK, LANES), F32),
            pltpu.VMEM((PEER_TOPK, LANES), F32),
            pltpu.VMEM((_STAIR_PAD, LANES), F32),
        ],
        compiler_params=_params("parallel"),
        name="peer_prep",
    )(xn, wq_t, sk)


def _peer_dense_kernel(xn_ref, u_ref, vt_ref, n_ref, ea_ref, rb_ref, eb_ref, h_ref, gf_ref,
                       o_ref, acc_ref, a_scr, *, n_i, apply_final):
    e = pl.program_id(1)

    @pl.when(e == 0)
    def _():
        acc_ref[...] = jnp.zeros_like(acc_ref)

    K = PEER_NKEYS
    hid = _dot_nt(u_ref[...], xn_ref[...])
    for ii in range(n_i):
        i = e * n_i + ii
        hh = hid[ii * K:(ii + 1) * K, :]
        act = 0.5 * hh * (1.0 + lax.erf(hh * (2.0 ** -0.5)))
        w = None
        for hd in range(PEER_HEADS):
            n_row = n_ref[hd, pl.ds(i, 1), :].astype(BF16)
            ea_row = ea_ref[hd, pl.ds(i, 1), :].astype(BF16)
            term = jnp.where(rb_ref[hd] < n_row, ea_row * eb_ref[hd], jnp.zeros((), BF16))
            w = term if w is None else w + term
        a_scr[ii * K:(ii + 1) * K, :] = act.astype(BF16) * w
    acc_ref[...] += _dot(vt_ref[...], a_scr[...])

    @pl.when(e == pl.num_programs(1) - 1)
    def _():
        h = h_ref[...] + acc_ref[...].T
        if apply_final:
            h = h * lax.rsqrt(jnp.mean(h * h, axis=-1, keepdims=True) + EPS) * gf_ref[...]
        o_ref[...] = h


def _peer_dense(xn, u, vt, n, ea, rb, eb, h1, gf, tb, eb_size, apply_final):
    T, D = xn.shape
    E = u.shape[0]
    H, K = PEER_HEADS, PEER_NKEYS
    sel = lambda t, e: (0, 0, t)
    tok = lambda t, e: (t, 0)
    return pl.pallas_call(
        functools.partial(_peer_dense_kernel, n_i=eb_size // K, apply_final=apply_final),
        grid=(T // tb, E // eb_size),
        in_specs=[
            pl.BlockSpec((tb, D), tok),
            pl.BlockSpec((eb_size, D), lambda t, e: (e, 0)),
            pl.BlockSpec((D, eb_size), lambda t, e: (0, e)),
            pl.BlockSpec((H, K, tb), sel),
            pl.BlockSpec((H, K, tb), sel),
            pl.BlockSpec((H, K, tb), sel),
            pl.BlockSpec((H, K, tb), sel),
            pl.BlockSpec((tb, D), tok),
            pl.BlockSpec((1, D), lambda t, e: (0, 0)),
        ],
        out_specs=pl.BlockSpec((tb, D), tok),
        out_shape=jax.ShapeDtypeStruct((T, D), F32),
        scratch_shapes=[pltpu.VMEM((D, tb), F32), pltpu.VMEM((eb_size, tb), BF16)],
        compiler_params=_params("parallel", "arbitrary"),
        name="peer_dense",
    )(xn, u, vt, n, ea, rb, eb, h1, gf)


def _pick(total, want):
    b = min(total, want)
    while total % b:
        b //= 2
    return b


def kernel(x, norm1_g, w_in, ret_gn_g, swa_sinks, rel_bias, w_out, norm2_g,
           peer_wq, peer_subkeys, peer_u, peer_v, final_g):
    B, S, D = x.shape
    T = B * S
    depth = w_in.shape[0]
    n_exp = peer_u.shape[1]

    tm = _pick(S, 512)
    tc = _pick(S, 1024)
    tq = _pick(S, 512)
    tb_prep = _pick(T, 256)
    tb = _pick(T, 512)
    eb_size = _pick(n_exp, 1024)

    inv = 1.0 / (ROPE_BASE ** (np.arange(0, RET_DK, 2, dtype=np.float32) / RET_DK))
    ang = np.arange(S, dtype=np.float32)[:, None] * inv[None, :]
    cos_t = jnp.asarray(np.tile(np.concatenate([np.cos(ang), np.cos(ang)], axis=1), (1, RET_HEADS)), F32)
    sin_t = jnp.asarray(np.tile(np.concatenate([-np.sin(ang), np.sin(ang)], axis=1), (1, RET_HEADS)), F32)

    bias = _swa_bias(rel_bias)
    gf = final_g.reshape(1, D).astype(F32)
    h = x.reshape(T, D)
    for l in range(depth):
        wl = w_in[l]
        v_off = 2 * RET_Q + 2 * RET_V + SWA_Q + SWA_KV
        v_cols = [wl[:, v_off + k * SWA_DH:v_off + (k + 1) * SWA_DH] for k in range(SWA_KV_HEADS)]
        w1 = jnp.concatenate([wl[:, :v_off]] + [c for c in v_cols for _ in range(2)], axis=1).astype(BF16)
        rq, rk, rv, rg, sq, sk, svv = _inproj(h, norm1_g[l].reshape(1, D), w1, cos_t, sin_t, S, tm)
        ret_out = _retention(rq, rk, rv, rg, ret_gn_g[l].reshape(1, RET_V), B, S, tc)
        swa_out = _swa(sq, sk, svv, bias, swa_sinks[l].astype(F32), B, S, tq)
        h1, xn = _outproj(h, ret_out, swa_out, w_out[l].astype(BF16), norm2_g[l].reshape(1, D), tm)
        wq_t = peer_wq[l].T.astype(BF16)
        skeys = peer_subkeys[l].reshape(2 * PEER_HEADS, PEER_NKEYS, PEER_DQ // 2).astype(BF16)
        n, ea, rb, eb = _peer_prep(xn, wq_t, skeys, tb_prep)
        u = peer_u[l].astype(BF16)
        vt = peer_v[l].T.astype(BF16)
        h = _peer_dense(xn, u, vt, n, ea, rb, eb, h1, gf, tb, eb_size, l == depth - 1)
    return h.reshape(B, S, D)
```

```python
import functools
import math

import numpy as np
import jax
import jax.numpy as jnp
from jax import lax
from jax.experimental import pallas as pl
from jax.experimental.pallas import tpu as pltpu

F32 = jnp.float32
BF16 = jnp.bfloat16

EPS = 1e-6
RET_HEADS = 4
RET_DK = 64
RET_DV = 128
RET_CHUNK = 128
ROPE_BASE = 10000.0
SWA_HEADS = 8
SWA_KV_HEADS = 2
SWA_DH = 64
WINDOW = 128
REL_BUCKETS = 32
REL_MAX_DIST = 128
PEER_HEADS = 8
PEER_NKEYS = 128
PEER_DQ = 256
PEER_TOPK = 16

RET_Q = RET_HEADS * RET_DK
RET_V = RET_HEADS * RET_DV
SWA_Q = SWA_HEADS * SWA_DH
SWA_KV = SWA_KV_HEADS * SWA_DH
SWA_GROUP = SWA_HEADS // SWA_KV_HEADS

LANES = 128
BF16_ROWS = 16
VMEM_LIMIT = 48 * 1024 * 1024
VMEM_LIMIT_DENSE = 56 * 1024 * 1024
DENSE_SUB_BLOCKS = 4
NEG = -1e30

_STAIR_COUNTS = [PEER_TOPK // (k + 1) for k in range(PEER_TOPK)]
_STAIR_OFFS = [int(v) for v in np.cumsum([0] + _STAIR_COUNTS[:-1])]
_STAIR_ROWS = int(sum(_STAIR_COUNTS))
_STAIR_PAD = -(-_STAIR_ROWS // 8) * 8


def _dot(a, b):
    return jnp.dot(a, b, preferred_element_type=F32)


def _dot_nt(a, b):
    return lax.dot_general(a, b, (((1,), (1,)), ((), ())), preferred_element_type=F32)


def _dot_tn(a, b):
    return lax.dot_general(a, b, (((0,), (0,)), ((), ())), preferred_element_type=F32)


def _params(*sem):
    return pltpu.CompilerParams(dimension_semantics=sem, vmem_limit_bytes=VMEM_LIMIT)


def _inproj_kernel(x_ref, g_ref, w_ref, cos_ref, sin_ref,
                   rq_ref, rk_ref, rv_ref, rg_ref, sq_ref, sk_ref, svv_ref):
    x = x_ref[...]
    y = x * lax.rsqrt(jnp.mean(x * x, axis=-1, keepdims=True) + EPS) * g_ref[...]
    proj = _dot(y.astype(BF16), w_ref[...])
    cos = cos_ref[...]
    sin = sin_ref[...]
    lane = lax.broadcasted_iota(jnp.int32, cos.shape, 1)
    first_half = (lane % RET_DK) < (RET_DK // 2)

    def rot(v):
        swapped = jnp.where(first_half, pltpu.roll(v, RET_Q - RET_DK // 2, 1),
                            pltpu.roll(v, RET_DK // 2, 1))
        return v * cos + swapped * sin

    c = 0
    rq = rot(proj[:, c:c + RET_Q]); c += RET_Q
    rk = rot(proj[:, c:c + RET_Q]) * (RET_DK ** -0.5); c += RET_Q
    for h in range(RET_HEADS):
        rq_ref[h] = rq[:, h * RET_DK:(h + 1) * RET_DK]
        rk_ref[h] = rk[:, h * RET_DK:(h + 1) * RET_DK]
    rv_ref[...] = proj[:, c:c + RET_V].astype(BF16); c += RET_V
    rg_ref[...] = proj[:, c:c + RET_V]; c += RET_V
    for h in range(SWA_HEADS):
        sq_ref[h] = proj[:, c + h * SWA_DH:c + (h + 1) * SWA_DH].astype(BF16)
    c += SWA_Q
    for h in range(SWA_KV_HEADS):
        sk_ref[h] = proj[:, c + h * SWA_DH:c + (h + 1) * SWA_DH].astype(BF16)
    c += SWA_KV
    svv_ref[...] = proj[:, c:c + 2 * SWA_KV].astype(BF16)


def _inproj(x2, g1, w1, cos_t, sin_t, S, tm):
    T, D = x2.shape
    n_s = S // tm
    ncol = w1.shape[1]
    tok = lambda i: (i, 0)
    hm = lambda i: (0, i, 0)
    return pl.pallas_call(
        _inproj_kernel,
        grid=(T // tm,),
        in_specs=[
            pl.BlockSpec((tm, D), tok),
            pl.BlockSpec((1, D), lambda i: (0, 0)),
            pl.BlockSpec((D, ncol), lambda i: (0, 0)),
            pl.BlockSpec((tm, RET_Q), lambda i: (i % n_s, 0)),
            pl.BlockSpec((tm, RET_Q), lambda i: (i % n_s, 0)),
        ],
        out_specs=[
            pl.BlockSpec((RET_HEADS, tm, RET_DK), hm),
            pl.BlockSpec((RET_HEADS, tm, RET_DK), hm),
            pl.BlockSpec((tm, RET_V), tok),
            pl.BlockSpec((tm, RET_V), tok),
            pl.BlockSpec((SWA_HEADS, tm, SWA_DH), hm),
            pl.BlockSpec((SWA_KV_HEADS, tm, SWA_DH), hm),
            pl.BlockSpec((tm, 2 * SWA_KV), tok),
        ],
        out_shape=[
            jax.ShapeDtypeStruct((RET_HEADS, T, RET_DK), F32),
            jax.ShapeDtypeStruct((RET_HEADS, T, RET_DK), F32),
            jax.ShapeDtypeStruct((T, RET_V), BF16),
            jax.ShapeDtypeStruct((T, RET_V), F32),
            jax.ShapeDtypeStruct((SWA_HEADS, T, SWA_DH), BF16),
            jax.ShapeDtypeStruct((SWA_KV_HEADS, T, SWA_DH), BF16),
            jax.ShapeDtypeStruct((T, 2 * SWA_KV), BF16),
        ],
        compiler_params=_params("parallel"),
        name="inproj",
    )(x2, g1, w1, cos_t, sin_t)


def _retention_kernel(q_ref, k_ref, v_ref, g_ref, dm_ref, xi_ref, zeta_ref, cd_ref, gn_ref,
                      o_ref, state_ref, *, n_chunks):
    @pl.when(pl.program_id(2) == 0)
    def _():
        state_ref[...] = jnp.zeros_like(state_ref)

    dm = dm_ref[0]
    xi = xi_ref[0]
    zeta = zeta_ref[0]
    cd = cd_ref[0]
    gn = gn_ref[...]

    def chunk(c, carry):
        r = pl.multiple_of(c * RET_CHUNK, RET_CHUNK)
        q = q_ref[0, pl.ds(r, RET_CHUNK), :]
        k = k_ref[0, pl.ds(r, RET_CHUNK), :]
        v = v_ref[pl.ds(r, RET_CHUNK), :]
        g = g_ref[pl.ds(r, RET_CHUNK), :]
        st = state_ref[...]
        s = _dot_nt(q.astype(BF16), k.astype(BF16)) * dm
        o = _dot(s.astype(BF16), v) + _dot((q * xi).astype(BF16), st.astype(BF16))
        state_ref[...] = st * cd + _dot_tn((k * zeta).astype(BF16), v)
        mu = jnp.mean(o, axis=-1, keepdims=True)
        oc = o - mu
        var = jnp.mean(oc * oc, axis=-1, keepdims=True)
        on = oc * lax.rsqrt(var + EPS) * gn
        o_ref[pl.ds(r, RET_CHUNK), :] = (g * jax.nn.sigmoid(g)) * on
        return carry

    lax.fori_loop(0, n_chunks, chunk, 0)


def _retention(rq, rk, rv, rg, gn, B, S, tc):
    T = rv.shape[0]
    n_c = S // tc
    C = RET_CHUNK
    lg = np.log(1.0 - 2.0 ** (-5.0 - np.arange(RET_HEADS, dtype=np.float64)))
    idx = np.arange(C, dtype=np.float64)
    diff = idx[:, None] - idx[None, :]
    dm = np.where(diff[None] >= 0, np.exp(np.maximum(diff, 0.0)[None] * lg[:, None, None]), 0.0)
    xi = np.broadcast_to(np.exp((idx + 1.0)[None, :] * lg[:, None])[:, :, None], (RET_HEADS, C, RET_DK))
    zeta = np.broadcast_to(np.exp((C - 1.0 - idx)[None, :] * lg[:, None])[:, :, None], (RET_HEADS, C, RET_DK))
    cd = np.broadcast_to(np.exp(C * lg)[:, None, None], (RET_HEADS, RET_DK, RET_DV))
    const = lambda a: jnp.asarray(np.ascontiguousarray(a), F32)
    hm = lambda b, h, c: (h, b * n_c + c, 0)
    tokh = lambda b, h, c: (b * n_c + c, h)
    perh = lambda b, h, c: (h, 0, 0)
    return pl.pallas_call(
        functools.partial(_retention_kernel, n_chunks=tc // C),
        grid=(B, RET_HEADS, n_c),
        in_specs=[
            pl.BlockSpec((1, tc, RET_DK), hm),
            pl.BlockSpec((1, tc, RET_DK), hm),
            pl.BlockSpec((tc, RET_DV), tokh),
            pl.BlockSpec((tc, RET_DV), tokh),
            pl.BlockSpec((1, C, C), perh),
            pl.BlockSpec((1, C, RET_DK), perh),
            pl.BlockSpec((1, C, RET_DK), perh),
            pl.BlockSpec((1, RET_DK, RET_DV), perh),
            pl.BlockSpec((1, RET_DV), lambda b, h, c: (0, h)),
        ],
        out_specs=pl.BlockSpec((tc, RET_DV), tokh),
        out_shape=jax.ShapeDtypeStruct((T, RET_V), F32),
        scratch_shapes=[pltpu.VMEM((RET_DK, RET_DV), F32)],
        compiler_params=_params("parallel", "parallel", "arbitrary"),
        name="retention",
    )(rq, rk, rv, rg, const(dm), const(xi), const(zeta), const(cd), gn)


def _t5_bucket(rel):
    max_exact = REL_BUCKETS // 2
    n = np.maximum(rel, 0)
    large = max_exact + (np.log(np.maximum(n, 1).astype(np.float32) / max_exact)
                         / math.log(REL_MAX_DIST / max_exact)
                         * (REL_BUCKETS - max_exact)).astype(np.int32)
    large = np.minimum(large, REL_BUCKETS - 1)
    return np.where(n < max_exact, n, large).astype(np.int32)


def _swa_bias_kernel(bucket_ref, rb_ref, o_ref):
    bucket = bucket_ref[...]
    for h in range(SWA_HEADS):
        acc = jnp.zeros(bucket.shape, F32)
        for b in range(REL_BUCKETS):
            acc = jnp.where(bucket == b, rb_ref[b, h], acc)
        o_ref[h] = acc


def _swa_bias(rel_bias):
    W = WINDOW
    rel = np.arange(W)[:, None] + W - np.arange(2 * W)[None, :]
    bucket = jnp.asarray(_t5_bucket(rel), jnp.int32)
    return pl.pallas_call(
        _swa_bias_kernel,
        in_specs=[pl.BlockSpec(memory_space=pltpu.VMEM), pl.BlockSpec(memory_space=pltpu.SMEM)],
        out_specs=pl.BlockSpec(memory_space=pltpu.VMEM),
        out_shape=jax.ShapeDtypeStruct((SWA_HEADS, W, 2 * W), F32),
        name="swa_bias",
    )(bucket, rel_bias.astype(F32))


def _swa_kernel(sink_ref, q_ref, kc_ref, kp_ref, vc_ref, vp_ref, bias_ref, o_ref, *, n_sub):
    W = WINDOW
    kvh = pl.program_id(1)
    first_step = pl.program_id(2) == 0
    qi = lax.broadcasted_iota(jnp.int32, (W, 2 * W), 0)
    kj = lax.broadcasted_iota(jnp.int32, (W, 2 * W), 1)
    rel = qi + W - kj
    band = (rel >= 0) & (rel < W)
    lane = lax.broadcasted_iota(jnp.int32, (2 * W, 2 * SWA_DH), 1)
    lo = lane < SWA_DH
    scale = SWA_DH ** -0.5
    for j in range(n_sub):
        if j == 0:
            kp = kp_ref[0]
            vp = vp_ref[...]
            valid = band & ((kj >= W) | jnp.logical_not(first_step))
        else:
            kp = kc_ref[0, (j - 1) * W:j * W, :]
            vp = vc_ref[(j - 1) * W:j * W, :]
            valid = band
        kw = jnp.concatenate([kp, kc_ref[0, j * W:(j + 1) * W, :]], axis=0)
        vw = jnp.concatenate([vp, vc_ref[j * W:(j + 1) * W, :]], axis=0)
        v_lo = jnp.where(lo, vw, jnp.zeros_like(vw))
        v_hi = jnp.where(lo, jnp.zeros_like(vw), vw)
        for pair in range(SWA_GROUP // 2):
            acc = None
            for sub, v_half in ((0, v_lo), (1, v_hi)):
                g = 2 * pair + sub
                q = q_ref[g, j * W:(j + 1) * W, :]
                s = _dot_nt(q, kw) * scale + bias_ref[g]
                s = jnp.where(valid, s, NEG)
                sink = sink_ref[kvh * SWA_GROUP + g]
                m = jnp.maximum(jnp.max(s, axis=-1, keepdims=True), sink)
                p = jnp.exp(s - m)
                den = jnp.sum(p, axis=-1, keepdims=True) + jnp.exp(sink - m)
                p = p / den
                o = _dot(p.astype(BF16), v_half)
                acc = o if acc is None else acc + o
            o_ref[j * W:(j + 1) * W, pair * 2 * SWA_DH:(pair + 1) * 2 * SWA_DH] = acc


def _swa(sq, sk, svv, bias, sinks, B, S, tq):
    T = svv.shape[0]
    W = WINDOW
    n_q = S // tq
    sub = tq // W
    cur3 = lambda b, h, i: (h, b * n_q + i, 0)
    prev3 = lambda b, h, i: (h, jnp.maximum(b * (S // W) + i * sub - 1, b * (S // W)), 0)
    cur2 = lambda b, h, i: (b * n_q + i, h)
    prev2 = lambda b, h, i: (jnp.maximum(b * (S // W) + i * sub - 1, b * (S // W)), h)
    return pl.pallas_call(
        functools.partial(_swa_kernel, n_sub=sub),
        grid=(B, SWA_KV_HEADS, n_q),
        in_specs=[
            pl.BlockSpec(memory_space=pltpu.SMEM),
            pl.BlockSpec((SWA_GROUP, tq, SWA_DH), cur3),
            pl.BlockSpec((1, tq, SWA_DH), cur3),
            pl.BlockSpec((1, W, SWA_DH), prev3),
            pl.BlockSpec((tq, 2 * SWA_DH), cur2),
            pl.BlockSpec((W, 2 * SWA_DH), prev2),
            pl.BlockSpec((SWA_GROUP, W, 2 * W), lambda b, h, i: (h, 0, 0)),
        ],
        out_specs=pl.BlockSpec((tq, SWA_GROUP * SWA_DH), cur2),
        out_shape=jax.ShapeDtypeStruct((T, SWA_Q), F32),
        compiler_params=_params("parallel", "parallel", "arbitrary"),
        name="swa",
    )(sinks, sq, sk, sk, svv, svv, bias)


def _outproj_kernel(x_ref, ret_ref, swa_ref, w_ref, g_ref, h_ref, xn_ref):
    h = (x_ref[...]
         + _dot(ret_ref[...].astype(BF16), w_ref[0:RET_V, :])
         + _dot(swa_ref[...].astype(BF16), w_ref[RET_V:RET_V + SWA_Q, :]))
    h_ref[...] = h
    y = h * lax.rsqrt(jnp.mean(h * h, axis=-1, keepdims=True) + EPS) * g_ref[...]
    xn_ref[...] = y.T.astype(BF16)


def _outproj(x2, ret_out, swa_out, w_out, g2, tm):
    T, D = x2.shape
    tok = lambda i: (i, 0)
    return pl.pallas_call(
        _outproj_kernel,
        grid=(T // tm,),
        in_specs=[
            pl.BlockSpec((tm, D), tok),
            pl.BlockSpec((tm, RET_V), tok),
            pl.BlockSpec((tm, SWA_Q), tok),
            pl.BlockSpec(w_out.shape, lambda i: (0, 0)),
            pl.BlockSpec((1, D), lambda i: (0, 0)),
        ],
        out_specs=[pl.BlockSpec((tm, D), tok), pl.BlockSpec((D, tm), lambda i: (0, i))],
        out_shape=[jax.ShapeDtypeStruct((T, D), F32), jax.ShapeDtypeStruct((D, T), BF16)],
        compiler_params=_params("parallel"),
        name="outproj",
    )(x2, ret_out, swa_out, w_out, g2)


def _extract_top(x, with_rank):
    rank = jnp.full(x.shape, float(PEER_NKEYS - 1), F32) if with_rank else None
    vals = []
    for r in range(PEER_TOPK):
        m = jnp.max(x, axis=0, keepdims=True)
        is_max = x == m
        if with_rank:
            rank = jnp.where(is_max, float(r), rank)
        x = jnp.where(is_max, -jnp.inf, x)
        vals.append(m)
    return vals, rank


def _peer_prep_kernel(xn_ref, wq_ref, sk_ref, n_ref, ea_ref, rb_ref, eb_ref,
                      q_scr, am_scr, bm_scr, cand_scr, *, n_lane_tiles):
    half = PEER_DQ // 2
    q_scr[...] = _dot(wq_ref[...], xn_ref[...]).astype(BF16)

    def head(h, carry):
        for lt in range(n_lane_tiles):
            cols = pl.ds(lt * LANES, LANES)
            ra = pl.multiple_of(h * PEER_DQ, PEER_DQ)
            a = _dot(sk_ref[2 * h], q_scr[pl.ds(ra, half), cols])
            b = _dot(sk_ref[2 * h + 1], q_scr[pl.ds(ra + half, half), cols])
            avals, _ = _extract_top(a, False)
            bvals, rb = _extract_top(b, True)
            for r in range(PEER_TOPK):
                am_scr[r:r + 1, :] = avals[r]
                bm_scr[r:r + 1, :] = bvals[r]
            cand_scr[_STAIR_ROWS:_STAIR_PAD, :] = jnp.full((_STAIR_PAD - _STAIR_ROWS, LANES), -jnp.inf, F32)
            for k in range(PEER_TOPK):
                cnt = _STAIR_COUNTS[k]
                cand_scr[_STAIR_OFFS[k]:_STAIR_OFFS[k] + cnt, :] = avals[k] + bm_scr[0:cnt, :]
            cand = cand_scr[...]
            x = cand
            tau = None
            for r in range(PEER_TOPK):
                tau = jnp.max(x, axis=0, keepdims=True)
                x = jnp.where(x == tau, -jnp.inf, x)
            top = avals[0] + bvals[0]
            z = jnp.sum(jnp.where(cand >= tau, jnp.exp(cand - top), 0.0), axis=0, keepdims=True)
            n = jnp.zeros(a.shape, F32)
            for r in range(PEER_TOPK):
                n = n + jnp.where(a + bvals[r] >= tau, 1.0, 0.0)
            n_ref[h, :, cols] = n
            ea_ref[h, :, cols] = 0.5 * jnp.exp(a - avals[0])
            rb_ref[h, :, cols] = rb.astype(BF16)
            eb_ref[h, :, cols] = (jnp.exp(b - bvals[0]) / z).astype(BF16)
        return carry

    lax.fori_loop(0, PEER_HEADS, head, 0)


def _peer_prep(xn_t, wq_t, sk, tb):
    D, T = xn_t.shape
    H, K = PEER_HEADS, PEER_NKEYS
    blk = lambda i: (0, 0, i)
    return pl.pallas_call(
        functools.partial(_peer_prep_kernel, n_lane_tiles=tb // LANES),
        grid=(T // tb,),
        in_specs=[
            pl.BlockSpec((D, tb), lambda i: (0, i)),
            pl.BlockSpec(wq_t.shape, lambda i: (0, 0)),
            pl.BlockSpec(sk.shape, lambda i: (0, 0, 0)),
        ],
        out_specs=[pl.BlockSpec((H, K, tb), blk)] * 4,
        out_shape=[
            jax.ShapeDtypeStruct((H, K, T), F32),
            jax.ShapeDtypeStruct((H, K, T), F32),
            jax.ShapeDtypeStruct((H, K, T), BF16),
            jax.ShapeDtypeStruct((H, K, T), BF16),
        ],
        scratch_shapes=[
            pltpu.VMEM((H * PEER_DQ, tb), BF16),
            pltpu.VMEM((PEER_TOPK, LANES), F32),
            pltpu.VMEM((PEER_TOPK, LANES), F32),
            pltpu.VMEM((_STAIR_PAD, LANES), F32),
        ],
        compiler_params=_params("parallel"),
        name="peer_prep",
    )(xn_t, wq_t, sk)


def _peer_dense_kernel(xn_ref, u_ref, vtp_ref, vtc_ref, n_ref, ea_ref, rb_ref, eb_ref, h_ref, gf_ref,
                       o_ref, acc_ref, a0_scr, a1_scr, hid0_scr, hid1_scr, rb_scr, eb_scr,
                       *, n_i, n_steps, apply_final):
    g = pl.program_id(0)
    s = g % n_steps
    p_cur = (g // n_steps) % 2
    p_prev = (jnp.maximum(g - 1, 0) // n_steps) % 2

    @pl.when(g == 0)
    def _():
        acc_ref[...] = jnp.zeros_like(acc_ref)
        a1_scr[...] = jnp.zeros_like(a1_scr)

    @pl.when(s == 0)
    def _():
        rb_scr[...] = rb_ref[...]
        eb_scr[...] = eb_ref[...]

    K = PEER_NKEYS
    tb = xn_ref.shape[1]
    eb_size = n_i * K
    pk = (K // BF16_ROWS, BF16_ROWS, tb)

    n_sub = DENSE_SUB_BLOCKS
    sub = eb_size // n_sub
    d_rows = acc_ref.shape[1] // n_sub

    hid_scrs = (hid0_scr, hid1_scr)

    def first_matmul(blk, j):
        r0 = blk * eb_size + j * sub
        hid_scrs[(blk * n_sub + j) % 2][...] = _dot(u_ref[r0:r0 + sub, :], xn_ref[...])

    def second_matmul(vt_ref, a_scr, p, j):
        rows = slice(j * d_rows, (j + 1) * d_rows)
        acc_ref[p, rows, :] += _dot(vt_ref[rows, :], a_scr[...])

    def activations(blk, j, a_scr):
        hid = hid_scrs[(blk * n_sub + j) % 2]
        for ii in range(sub // K):
            i = (2 * s + blk) * n_i + j * (sub // K) + ii
            r0 = j * sub + ii * K
            n_rows = [n_ref[hd, pl.ds(i, 1), :] for hd in range(PEER_HEADS)]
            ea_rows = [ea_ref[hd, pl.ds(i, 1), :] for hd in range(PEER_HEADS)]
            for lt in range(tb // LANES):
                cols = slice(lt * LANES, (lt + 1) * LANES)
                hh = hid[ii * K:(ii + 1) * K, cols]
                act = hh + hh * lax.erf(hh * (2.0 ** -0.5))
                w = None
                for hd in range(PEER_HEADS):
                    n_row = jnp.broadcast_to(n_rows[hd][:, cols], (K, LANES)).astype(BF16)
                    ea_row = jnp.broadcast_to(ea_rows[hd][:, cols], (K, LANES)).astype(BF16)
                    keep = jnp.where(rb_scr[hd, :, cols] < n_row, eb_scr[hd, :, cols], jnp.zeros((), BF16))
                    w = ea_row * keep if w is None else w + ea_row * keep
                a_scr[r0:r0 + K, cols] = act.astype(BF16) * w

    def slot(*parts):
        for fn, args in parts:
            fn(*args)

    slot((first_matmul, (0, 0)), (second_matmul, (vtp_ref, a1_scr, p_prev, 0)))
    for j in range(1, n_sub):
        slot((first_matmul, (0, j)), (second_matmul, (vtp_ref, a1_scr, p_prev, j)),
             (activations, (0, j - 1, a0_scr)))
    slot((first_matmul, (1, 0)), (activations, (0, n_sub - 1, a0_scr)))
    for j in range(1, n_sub):
        slot((first_matmul, (1, j)), (second_matmul, (vtc_ref, a0_scr, p_cur, j - 1)),
             (activations, (1, j - 1, a1_scr)))
    slot((second_matmul, (vtc_ref, a0_scr, p_cur, n_sub - 1)), (activations, (1, n_sub - 1, a1_scr)))

    @pl.when((s == 0) & (g > 0))
    def _():
        h = h_ref[...] + acc_ref[p_prev].T
        acc_ref[p_prev] = jnp.zeros(acc_ref.shape[1:], F32)
        if apply_final:
            h = h * lax.rsqrt(jnp.mean(h * h, axis=-1, keepdims=True) + EPS) * gf_ref[...]
        o_ref[...] = h


def _peer_dense(xn_t, u, vt, n, ea, rb, eb, h1, gf, tb, eb_size, apply_final):
    D, T = xn_t.shape
    E = u.shape[0]
    H, K = PEER_HEADS, PEER_NKEYS
    n_t = T // tb
    n_steps = E // (2 * eb_size)
    assert n_steps >= 1 and E % (2 * eb_size) == 0
    cur_t = lambda g: jnp.minimum(g // n_steps, n_t - 1)
    done_t = lambda g: jnp.maximum(g - 1, 0) // n_steps
    sel = lambda g: (0, 0, cur_t(g))
    return pl.pallas_call(
        functools.partial(_peer_dense_kernel, n_i=eb_size // K, n_steps=n_steps, apply_final=apply_final),
        grid=(n_t * n_steps + 1,),
        in_specs=[
            pl.BlockSpec((D, tb), lambda g: (0, cur_t(g))),
            pl.BlockSpec((2 * eb_size, D), lambda g: (g % n_steps, 0)),
            pl.BlockSpec((D, eb_size), lambda g: (0, 2 * (jnp.maximum(g - 1, 0) % n_steps) + 1)),
            pl.BlockSpec((D, eb_size), lambda g: (0, 2 * (g % n_steps))),
            pl.BlockSpec((H, K, tb), sel),
            pl.BlockSpec((H, K, tb), sel),
            pl.BlockSpec((H, K, tb), sel),
            pl.BlockSpec((H, K, tb), sel),
            pl.BlockSpec((tb, D), lambda g: (done_t(g), 0)),
            pl.BlockSpec((1, D), lambda g: (0, 0)),
        ],
        out_specs=pl.BlockSpec((tb, D), lambda g: (done_t(g), 0)),
        out_shape=jax.ShapeDtypeStruct((T, D), F32),
        scratch_shapes=[pltpu.VMEM((2, D, tb), F32), pltpu.VMEM((eb_size, tb), BF16),
                        pltpu.VMEM((eb_size, tb), BF16),
                        pltpu.VMEM((eb_size // DENSE_SUB_BLOCKS, tb), F32),
                        pltpu.VMEM((eb_size // DENSE_SUB_BLOCKS, tb), F32),
                        pltpu.VMEM((H, K, tb), BF16), pltpu.VMEM((H, K, tb), BF16)],
        compiler_params=pltpu.CompilerParams(dimension_semantics=("arbitrary",),
                                             vmem_limit_bytes=VMEM_LIMIT_DENSE),
        name="peer_dense",
    )(xn_t, u, vt, vt, n, ea, rb, eb, h1, gf)


def _pick(total, want):
    b = min(total, want)
    while total % b:
        b //= 2
    return b


def kernel(x, norm1_g, w_in, ret_gn_g, swa_sinks, rel_bias, w_out, norm2_g,
           peer_wq, peer_subkeys, peer_u, peer_v, final_g):
    B, S, D = x.shape
    T = B * S
    depth = w_in.shape[0]
    n_exp = peer_u.shape[1]

    tm = _pick(S, 512)
    tc = _pick(S, 1024)
    tq = _pick(S, 512)
    tb_prep = _pick(T, 256)
    tb = _pick(T, 512)
    eb_size = _pick(n_exp, 1024)

    inv = 1.0 / (ROPE_BASE ** (np.arange(0, RET_DK, 2, dtype=np.float32) / RET_DK))
    ang = np.arange(S, dtype=np.float64)[:, None] * inv.astype(np.float64)[None, :]
    cos_t = jnp.asarray(np.tile(np.concatenate([np.cos(ang), np.cos(ang)], axis=1), (1, RET_HEADS)), F32)
    sin_t = jnp.asarray(np.tile(np.concatenate([-np.sin(ang), np.sin(ang)], axis=1), (1, RET_HEADS)), F32)

    bias = _swa_bias(rel_bias)
    gf = final_g.reshape(1, D).astype(F32)
    h = x.reshape(T, D)
    for l in range(depth):
        wl = w_in[l]
        v_off = 2 * RET_Q + 2 * RET_V + SWA_Q + SWA_KV
        v_cols = [wl[:, v_off + k * SWA_DH:v_off + (k + 1) * SWA_DH] for k in range(SWA_KV_HEADS)]
        w1 = jnp.concatenate([wl[:, :v_off]] + [c for c in v_cols for _ in range(2)], axis=1).astype(BF16)
        rq, rk, rv, rg, sq, sk, svv = _inproj(h, norm1_g[l].reshape(1, D), w1, cos_t, sin_t, S, tm)
        ret_out = _retention(rq, rk, rv, rg, ret_gn_g[l].reshape(1, RET_V), B, S, tc)
        swa_out = _swa(sq, sk, svv, bias, swa_sinks[l].astype(F32), B, S, tq)
        h1, xn_t = _outproj(h, ret_out, swa_out, w_out[l].astype(BF16), norm2_g[l].reshape(1, D), tm)
        wq_t = peer_wq[l].T.astype(BF16)
        skeys = peer_subkeys[l].reshape(2 * PEER_HEADS, PEER_NKEYS, PEER_DQ // 2).astype(BF16)
        n, ea, rb, eb = _peer_prep(xn_t, wq_t, skeys, tb_prep)
        u = peer_u[l].astype(BF16)
        vt = peer_v[l].T.astype(BF16)
        h = _peer_dense(xn_t, u, vt, n, ea, rb, eb, h1, gf, tb, eb_size, l == depth - 1)
    return h.reshape(B, S, D)
```

```python
import functools
import math

import numpy as np
import jax
import jax.numpy as jnp
from jax import lax
from jax.experimental import pallas as pl
from jax.experimental.pallas import tpu as pltpu

F32 = jnp.float32
BF16 = jnp.bfloat16

EPS = 1e-6
RET_HEADS = 4
RET_DK = 64
RET_DV = 128
RET_CHUNK = 128
ROPE_BASE = 10000.0
SWA_HEADS = 8
SWA_KV_HEADS = 2
SWA_DH = 64
WINDOW = 128
REL_BUCKETS = 32
REL_MAX_DIST = 128
PEER_HEADS = 8
PEER_NKEYS = 128
PEER_DQ = 256
PEER_TOPK = 16

RET_Q = RET_HEADS * RET_DK
RET_V = RET_HEADS * RET_DV
SWA_Q = SWA_HEADS * SWA_DH
SWA_KV = SWA_KV_HEADS * SWA_DH
SWA_GROUP = SWA_HEADS // SWA_KV_HEADS

LANES = 128
VMEM_LIMIT = 48 * 1024 * 1024
DENSE_ROW_CHUNK = 32
NEG = -1e30

_STAIR_COUNTS = [PEER_TOPK // (k + 1) for k in range(PEER_TOPK)]
_STAIR_OFFS = [int(v) for v in np.cumsum([0] + _STAIR_COUNTS[:-1])]
_STAIR_ROWS = int(sum(_STAIR_COUNTS))
_STAIR_PAD = -(-_STAIR_ROWS // 8) * 8


def _dot(a, b):
    return jnp.dot(a, b, preferred_element_type=F32)


def _dot_nt(a, b):
    return lax.dot_general(a, b, (((1,), (1,)), ((), ())), preferred_element_type=F32)


def _dot_tn(a, b):
    return lax.dot_general(a, b, (((0,), (0,)), ((), ())), preferred_element_type=F32)


def _params(*sem):
    return pltpu.CompilerParams(dimension_semantics=sem, vmem_limit_bytes=VMEM_LIMIT)


def _inproj_kernel(x_ref, g_ref, w_ref, cos_ref, sin_ref,
                   rq_ref, rk_ref, rv_ref, rg_ref, sq_ref, sk_ref, svv_ref):
    x = x_ref[...]
    y = x * lax.rsqrt(jnp.mean(x * x, axis=-1, keepdims=True) + EPS) * g_ref[...]
    proj = _dot(y.astype(BF16), w_ref[...])
    cos = cos_ref[...]
    sin = sin_ref[...]
    lane = lax.broadcasted_iota(jnp.int32, cos.shape, 1)
    first_half = (lane % RET_DK) < (RET_DK // 2)

    def rot(v):
        swapped = jnp.where(first_half, pltpu.roll(v, RET_Q - RET_DK // 2, 1),
                            pltpu.roll(v, RET_DK // 2, 1))
        return v * cos + swapped * sin

    c = 0
    rq = rot(proj[:, c:c + RET_Q]); c += RET_Q
    rk = rot(proj[:, c:c + RET_Q]) * (RET_DK ** -0.5); c += RET_Q
    for h in range(RET_HEADS):
        rq_ref[h] = rq[:, h * RET_DK:(h + 1) * RET_DK]
        rk_ref[h] = rk[:, h * RET_DK:(h + 1) * RET_DK]
    rv_ref[...] = proj[:, c:c + RET_V].astype(BF16); c += RET_V
    rg_ref[...] = proj[:, c:c + RET_V]; c += RET_V
    for h in range(SWA_HEADS):
        sq_ref[h] = proj[:, c + h * SWA_DH:c + (h + 1) * SWA_DH].astype(BF16)
    c += SWA_Q
    for h in range(SWA_KV_HEADS):
        sk_ref[h] = proj[:, c + h * SWA_DH:c + (h + 1) * SWA_DH].astype(BF16)
    c += SWA_KV
    svv_ref[...] = proj[:, c:c + 2 * SWA_KV].astype(BF16)


def _inproj(x2, g1, w1, cos_t, sin_t, S, tm):
    T, D = x2.shape
    n_s = S // tm
    ncol = w1.shape[1]
    tok = lambda i: (i, 0)
    hm = lambda i: (0, i, 0)
    return pl.pallas_call(
        _inproj_kernel,
        grid=(T // tm,),
        in_specs=[
            pl.BlockSpec((tm, D), tok),
            pl.BlockSpec((1, D), lambda i: (0, 0)),
            pl.BlockSpec((D, ncol), lambda i: (0, 0)),
            pl.BlockSpec((tm, RET_Q), lambda i: (i % n_s, 0)),
            pl.BlockSpec((tm, RET_Q), lambda i: (i % n_s, 0)),
        ],
        out_specs=[
            pl.BlockSpec((RET_HEADS, tm, RET_DK), hm),
            pl.BlockSpec((RET_HEADS, tm, RET_DK), hm),
            pl.BlockSpec((tm, RET_V), tok),
            pl.BlockSpec((tm, RET_V), tok),
            pl.BlockSpec((SWA_HEADS, tm, SWA_DH), hm),
            pl.BlockSpec((SWA_KV_HEADS, tm, SWA_DH), hm),
            pl.BlockSpec((tm, 2 * SWA_KV), tok),
        ],
        out_shape=[
            jax.ShapeDtypeStruct((RET_HEADS, T, RET_DK), F32),
            jax.ShapeDtypeStruct((RET_HEADS, T, RET_DK), F32),
            jax.ShapeDtypeStruct((T, RET_V), BF16),
            jax.ShapeDtypeStruct((T, RET_V), F32),
            jax.ShapeDtypeStruct((SWA_HEADS, T, SWA_DH), BF16),
            jax.ShapeDtypeStruct((SWA_KV_HEADS, T, SWA_DH), BF16),
            jax.ShapeDtypeStruct((T, 2 * SWA_KV), BF16),
        ],
        compiler_params=_params("parallel"),
        name="inproj",
    )(x2, g1, w1, cos_t, sin_t)


def _retention_kernel(q_ref, k_ref, v0_ref, v1_ref, v2_ref, v3_ref, g0_ref, g1_ref, g2_ref, g3_ref,
                      dm_ref, xi_ref, zeta_ref, cd_ref, gn_ref, o_ref, state_ref, *, n_chunks):
    v_refs = (v0_ref, v1_ref, v2_ref, v3_ref)
    g_refs = (g0_ref, g1_ref, g2_ref, g3_ref)

    @pl.when(pl.program_id(1) == 0)
    def _():
        state_ref[...] = jnp.zeros_like(state_ref)

    def chunk(c, carry):
        r = pl.multiple_of(c * RET_CHUNK, RET_CHUNK)
        for h in range(RET_HEADS):
            q = q_ref[h, pl.ds(r, RET_CHUNK), :]
            k = k_ref[h, pl.ds(r, RET_CHUNK), :]
            v = v_refs[h][pl.ds(r, RET_CHUNK), :]
            g = g_refs[h][pl.ds(r, RET_CHUNK), :]
            st = state_ref[h]
            s = _dot_nt(q.astype(BF16), k.astype(BF16)) * dm_ref[h]
            o = _dot(s.astype(BF16), v) + _dot((q * xi_ref[h]).astype(BF16), st.astype(BF16))
            state_ref[h] = st * cd_ref[h] + _dot_tn((k * zeta_ref[h]).astype(BF16), v)
            mu = jnp.mean(o, axis=-1, keepdims=True)
            oc = o - mu
            var = jnp.mean(oc * oc, axis=-1, keepdims=True)
            on = oc * lax.rsqrt(var + EPS) * gn_ref[:, h * RET_DV:(h + 1) * RET_DV]
            o_ref[h, pl.ds(r, RET_CHUNK), :] = ((g * jax.nn.sigmoid(g)) * on).astype(BF16)
        return carry

    lax.fori_loop(0, n_chunks, chunk, 0)


def _retention(rq, rk, rv, rg, gn, B, S, tc):
    T = rv.shape[0]
    n_c = S // tc
    C = RET_CHUNK
    H = RET_HEADS
    lg = np.log(1.0 - 2.0 ** (-5.0 - np.arange(H, dtype=np.float64)))
    idx = np.arange(C, dtype=np.float64)
    diff = idx[:, None] - idx[None, :]
    dm = np.where(diff[None] >= 0, np.exp(np.maximum(diff, 0.0)[None] * lg[:, None, None]), 0.0)
    xi = np.broadcast_to(np.exp((idx + 1.0)[None, :] * lg[:, None])[:, :, None], (H, C, RET_DK))
    zeta = np.broadcast_to(np.exp((C - 1.0 - idx)[None, :] * lg[:, None])[:, :, None], (H, C, RET_DK))
    cd = np.broadcast_to(np.exp(C * lg)[:, None, None], (H, RET_DK, RET_DV))
    const = lambda a: jnp.asarray(np.ascontiguousarray(a), F32)
    hm = lambda b, c: (0, b * n_c + c, 0)
    full3 = lambda b, c: (0, 0, 0)
    per_head = [pl.BlockSpec((tc, RET_DV), functools.partial(lambda b, c, h: (b * n_c + c, h), h=h))
                for h in range(H)]
    return pl.pallas_call(
        functools.partial(_retention_kernel, n_chunks=tc // C),
        grid=(B, n_c),
        in_specs=[
            pl.BlockSpec((H, tc, RET_DK), hm),
            pl.BlockSpec((H, tc, RET_DK), hm),
            *per_head,
            *per_head,
            pl.BlockSpec((H, C, C), full3),
            pl.BlockSpec((H, C, RET_DK), full3),
            pl.BlockSpec((H, C, RET_DK), full3),
            pl.BlockSpec((H, RET_DK, RET_DV), full3),
            pl.BlockSpec((1, RET_V), lambda b, c: (0, 0)),
        ],
        out_specs=pl.BlockSpec((H, tc, RET_DV), hm),
        out_shape=jax.ShapeDtypeStruct((H, T, RET_DV), BF16),
        scratch_shapes=[pltpu.VMEM((H, RET_DK, RET_DV), F32)],
        compiler_params=_params("parallel", "arbitrary"),
        name="retention",
    )(rq, rk, rv, rv, rv, rv, rg, rg, rg, rg, const(dm), const(xi), const(zeta), const(cd), gn)


def _t5_bucket(rel):
    max_exact = REL_BUCKETS // 2
    n = np.maximum(rel, 0)
    large = max_exact + (np.log(np.maximum(n, 1).astype(np.float32) / max_exact)
                         / math.log(REL_MAX_DIST / max_exact)
                         * (REL_BUCKETS - max_exact)).astype(np.int32)
    large = np.minimum(large, REL_BUCKETS - 1)
    return np.where(n < max_exact, n, large).astype(np.int32)


def _swa_bias_kernel(bucket_ref, rb_ref, o_ref):
    bucket = bucket_ref[...]
    for h in range(SWA_HEADS):
        acc = jnp.zeros(bucket.shape, F32)
        for b in range(REL_BUCKETS):
            acc = jnp.where(bucket == b, rb_ref[b, h], acc)
        o_ref[h] = acc


def _swa_bias(rel_bias):
    W = WINDOW
    rel = np.arange(W)[:, None] + W - np.arange(2 * W)[None, :]
    bucket = jnp.asarray(_t5_bucket(rel), jnp.int32)
    return pl.pallas_call(
        _swa_bias_kernel,
        in_specs=[pl.BlockSpec(memory_space=pltpu.VMEM), pl.BlockSpec(memory_space=pltpu.SMEM)],
        out_specs=pl.BlockSpec(memory_space=pltpu.VMEM),
        out_shape=jax.ShapeDtypeStruct((SWA_HEADS, W, 2 * W), F32),
        name="swa_bias",
    )(bucket, rel_bias.astype(F32))


def _swa_kernel(sink_ref, q_ref, kc_ref, kp_ref, vc_ref, vp_ref, bias_ref, o_ref, *, n_sub):
    W = WINDOW
    kvh = pl.program_id(1)
    first_step = pl.program_id(2) == 0
    qi = lax.broadcasted_iota(jnp.int32, (W, 2 * W), 0)
    kj = lax.broadcasted_iota(jnp.int32, (W, 2 * W), 1)
    rel = qi + W - kj
    band = (rel >= 0) & (rel < W)
    lane = lax.broadcasted_iota(jnp.int32, (2 * W, 2 * SWA_DH), 1)
    lo = lane < SWA_DH
    scale = SWA_DH ** -0.5
    for j in range(n_sub):
        if j == 0:
            kp = kp_ref[0]
            vp = vp_ref[...]
            valid = band & ((kj >= W) | jnp.logical_not(first_step))
        else:
            kp = kc_ref[0, (j - 1) * W:j * W, :]
            vp = vc_ref[(j - 1) * W:j * W, :]
            valid = band
        kw = jnp.concatenate([kp, kc_ref[0, j * W:(j + 1) * W, :]], axis=0)
        vw = jnp.concatenate([vp, vc_ref[j * W:(j + 1) * W, :]], axis=0)
        v_lo = jnp.where(lo, vw, jnp.zeros_like(vw))
        v_hi = jnp.where(lo, jnp.zeros_like(vw), vw)
        for pair in range(SWA_GROUP // 2):
            acc = None
            for sub, v_half in ((0, v_lo), (1, v_hi)):
                g = 2 * pair + sub
                q = q_ref[g, j * W:(j + 1) * W, :]
                s = _dot_nt(q, kw) * scale + bias_ref[g]
                s = jnp.where(valid, s, NEG)
                sink = sink_ref[kvh * SWA_GROUP + g]
                m = jnp.maximum(jnp.max(s, axis=-1, keepdims=True), sink)
                p = jnp.exp(s - m)
                den = jnp.sum(p, axis=-1, keepdims=True) + jnp.exp(sink - m)
                p = p / den
                o = _dot(p.astype(BF16), v_half)
                acc = o if acc is None else acc + o
            o_ref[j * W:(j + 1) * W, pair * 2 * SWA_DH:(pair + 1) * 2 * SWA_DH] = acc.astype(BF16)


def _swa(sq, sk, svv, bias, sinks, B, S, tq):
    T = svv.shape[0]
    W = WINDOW
    n_q = S // tq
    sub = tq // W
    cur3 = lambda b, h, i: (h, b * n_q + i, 0)
    prev3 = lambda b, h, i: (h, jnp.maximum(b * (S // W) + i * sub - 1, b * (S // W)), 0)
    cur2 = lambda b, h, i: (b * n_q + i, h)
    prev2 = lambda b, h, i: (jnp.maximum(b * (S // W) + i * sub - 1, b * (S // W)), h)
    return pl.pallas_call(
        functools.partial(_swa_kernel, n_sub=sub),
        grid=(B, SWA_KV_HEADS, n_q),
        in_specs=[
            pl.BlockSpec(memory_space=pltpu.SMEM),
            pl.BlockSpec((SWA_GROUP, tq, SWA_DH), cur3),
            pl.BlockSpec((1, tq, SWA_DH), cur3),
            pl.BlockSpec((1, W, SWA_DH), prev3),
            pl.BlockSpec((tq, 2 * SWA_DH), cur2),
            pl.BlockSpec((W, 2 * SWA_DH), prev2),
            pl.BlockSpec((SWA_GROUP, W, 2 * W), lambda b, h, i: (h, 0, 0)),
        ],
        out_specs=pl.BlockSpec((tq, SWA_GROUP * SWA_DH), cur2),
        out_shape=jax.ShapeDtypeStruct((T, SWA_Q), BF16),
        compiler_params=_params("parallel", "parallel", "arbitrary"),
        name="swa",
    )(sinks, sq, sk, sk, svv, svv, bias)


def _outproj_kernel(x_ref, ret_ref, swa_ref, w_ref, g_ref, h_ref, xn_ref):
    h = x_ref[...] + _dot(swa_ref[...], w_ref[RET_V:RET_V + SWA_Q, :])
    for hd in range(RET_HEADS):
        h = h + _dot(ret_ref[hd], w_ref[hd * RET_DV:(hd + 1) * RET_DV, :])
    h_ref[...] = h
    y = h * lax.rsqrt(jnp.mean(h * h, axis=-1, keepdims=True) + EPS) * g_ref[...]
    xn_ref[...] = y.T.astype(BF16)


def _outproj(x2, ret_out, swa_out, w_out, g2, tm):
    T, D = x2.shape
    tok = lambda i: (i, 0)
    return pl.pallas_call(
        _outproj_kernel,
        grid=(T // tm,),
        in_specs=[
            pl.BlockSpec((tm, D), tok),
            pl.BlockSpec((RET_HEADS, tm, RET_DV), lambda i: (0, i, 0)),
            pl.BlockSpec((tm, SWA_Q), tok),
            pl.BlockSpec(w_out.shape, lambda i: (0, 0)),
            pl.BlockSpec((1, D), lambda i: (0, 0)),
        ],
        out_specs=[pl.BlockSpec((tm, D), tok), pl.BlockSpec((D, tm), lambda i: (0, i))],
        out_shape=[jax.ShapeDtypeStruct((T, D), F32), jax.ShapeDtypeStruct((D, T), BF16)],
        compiler_params=_params("parallel"),
        name="outproj",
    )(x2, ret_out, swa_out, w_out, g2)


def _extract_top(x, with_rank):
    rank = jnp.full(x.shape, float(PEER_NKEYS - 1), F32) if with_rank else None
    vals = []
    for r in range(PEER_TOPK):
        m = jnp.max(x, axis=0, keepdims=True)
        is_max = x == m
        if with_rank:
            rank = jnp.where(is_max, float(r), rank)
        x = jnp.where(is_max, -jnp.inf, x)
        vals.append(m)
    return vals, rank


def _peer_prep_kernel(xn_ref, wq_ref, sk_ref, n_ref, ea_ref, rb_ref, eb_ref,
                      q_scr, am_scr, bm_scr, cand_scr, *, n_lane_tiles):
    half = PEER_DQ // 2
    q_scr[...] = _dot(wq_ref[...], xn_ref[...]).astype(BF16)

    def head(h, carry):
        for lt in range(n_lane_tiles):
            cols = pl.ds(lt * LANES, LANES)
            ra = pl.multiple_of(h * PEER_DQ, PEER_DQ)
            a = _dot(sk_ref[2 * h], q_scr[pl.ds(ra, half), cols])
            b = _dot(sk_ref[2 * h + 1], q_scr[pl.ds(ra + half, half), cols])
            avals, _ = _extract_top(a, False)
            bvals, rb = _extract_top(b, True)
            for r in range(PEER_TOPK):
                am_scr[r:r + 1, :] = avals[r]
                bm_scr[r:r + 1, :] = bvals[r]
            cand_scr[_STAIR_ROWS:_STAIR_PAD, :] = jnp.full((_STAIR_PAD - _STAIR_ROWS, LANES), -jnp.inf, F32)
            for k in range(PEER_TOPK):
                cnt = _STAIR_COUNTS[k]
                cand_scr[_STAIR_OFFS[k]:_STAIR_OFFS[k] + cnt, :] = avals[k] + bm_scr[0:cnt, :]
            cand = cand_scr[...]
            x = cand
            tau = None
            for r in range(PEER_TOPK):
                tau = jnp.max(x, axis=0, keepdims=True)
                x = jnp.where(x == tau, -jnp.inf, x)
            top = avals[0] + bvals[0]
            z = jnp.sum(jnp.where(cand >= tau, jnp.exp(cand - top), 0.0), axis=0, keepdims=True)
            n = jnp.zeros(a.shape, F32)
            for r in range(PEER_TOPK):
                n = n + jnp.where(a + bvals[r] >= tau, 1.0, 0.0)
            n_ref[h, :, cols] = n
            ea_ref[h, :, cols] = 0.5 * jnp.exp(a - avals[0])
            rb_ref[h, :, cols] = rb.astype(BF16)
            eb_ref[h, :, cols] = (jnp.exp(b - bvals[0]) / z).astype(BF16)
        return carry

    lax.fori_loop(0, PEER_HEADS, head, 0)


def _peer_prep(xn_t, wq_t, sk, tb):
    D, T = xn_t.shape
    H, K = PEER_HEADS, PEER_NKEYS
    blk = lambda i: (0, 0, i)
    return pl.pallas_call(
        functools.partial(_peer_prep_kernel, n_lane_tiles=tb // LANES),
        grid=(T // tb,),
        in_specs=[
            pl.BlockSpec((D, tb), lambda i: (0, i)),
            pl.BlockSpec(wq_t.shape, lambda i: (0, 0)),
            pl.BlockSpec(sk.shape, lambda i: (0, 0, 0)),
        ],
        out_specs=[pl.BlockSpec((H, K, tb), blk)] * 4,
        out_shape=[
            jax.ShapeDtypeStruct((H, K, T), F32),
            jax.ShapeDtypeStruct((H, K, T), F32),
            jax.ShapeDtypeStruct((H, K, T), BF16),
            jax.ShapeDtypeStruct((H, K, T), BF16),
        ],
        scratch_shapes=[
            pltpu.VMEM((H * PEER_DQ, tb), BF16),
            pltpu.VMEM((PEER_TOPK, LANES), F32),
            pltpu.VMEM((PEER_TOPK, LANES), F32),
            pltpu.VMEM((_STAIR_PAD, LANES), F32),
        ],
        compiler_params=_params("parallel"),
        name="peer_prep",
    )(xn_t, wq_t, sk)


def _peer_dense_kernel(xn_ref, u_ref, vt_ref, n_ref, ea_ref, rb_ref, eb_ref, h_ref, gf_ref,
                       o_ref, acc_ref, a_scr, rb_scr, eb_scr, *, n_i, apply_final):
    e = pl.program_id(1)

    @pl.when(e == 0)
    def _():
        acc_ref[...] = jnp.zeros_like(acc_ref)
        rb_scr[...] = rb_ref[...]
        eb_scr[...] = eb_ref[...]

    K = PEER_NKEYS
    tb = xn_ref.shape[1]
    hid = _dot(u_ref[...], xn_ref[...])
    for ii in range(n_i):
        i = e * n_i + ii
        n_rows = [n_ref[hd, pl.ds(i, 1), :] for hd in range(PEER_HEADS)]
        ea_rows = [ea_ref[hd, pl.ds(i, 1), :] for hd in range(PEER_HEADS)]
        n_bc = [jnp.broadcast_to(r, (DENSE_ROW_CHUNK, tb)).astype(BF16) for r in n_rows]
        ea_bc = [jnp.broadcast_to(r, (DENSE_ROW_CHUNK, tb)).astype(BF16) for r in ea_rows]
        for rc in range(K // DENSE_ROW_CHUNK):
            rows = slice(rc * DENSE_ROW_CHUNK, (rc + 1) * DENSE_ROW_CHUNK)
            hh = hid[ii * K + rc * DENSE_ROW_CHUNK:ii * K + (rc + 1) * DENSE_ROW_CHUNK, :]
            act = hh + hh * lax.erf(hh * (2.0 ** -0.5))
            w = None
            for hd in range(PEER_HEADS):
                keep = jnp.where(rb_scr[hd, rows, :] < n_bc[hd], eb_scr[hd, rows, :], jnp.zeros((), BF16))
                w = ea_bc[hd] * keep if w is None else w + ea_bc[hd] * keep
            r0 = ii * K + rc * DENSE_ROW_CHUNK
            a_scr[r0:r0 + DENSE_ROW_CHUNK, :] = act.astype(BF16) * w
    acc_ref[...] += _dot(vt_ref[...], a_scr[...])

    @pl.when(e == pl.num_programs(1) - 1)
    def _():
        h = h_ref[...] + acc_ref[...].T
        if apply_final:
            h = h * lax.rsqrt(jnp.mean(h * h, axis=-1, keepdims=True) + EPS) * gf_ref[...]
        o_ref[...] = h


def _peer_dense(xn_t, u, vt, n, ea, rb, eb, h1, gf, tb, eb_size, apply_final):
    D, T = xn_t.shape
    E = u.shape[0]
    H, K = PEER_HEADS, PEER_NKEYS
    sel = lambda t, e: (0, 0, t)
    tok = lambda t, e: (t, 0)
    return pl.pallas_call(
        functools.partial(_peer_dense_kernel, n_i=eb_size // K, apply_final=apply_final),
        grid=(T // tb, E // eb_size),
        in_specs=[
            pl.BlockSpec((D, tb), lambda t, e: (0, t)),
            pl.BlockSpec((eb_size, D), lambda t, e: (e, 0)),
            pl.BlockSpec((D, eb_size), lambda t, e: (0, e)),
            pl.BlockSpec((H, K, tb), sel),
            pl.BlockSpec((H, K, tb), sel),
            pl.BlockSpec((H, K, tb), sel),
            pl.BlockSpec((H, K, tb), sel),
            pl.BlockSpec((tb, D), tok),
            pl.BlockSpec((1, D), lambda t, e: (0, 0)),
        ],
        out_specs=pl.BlockSpec((tb, D), tok),
        out_shape=jax.ShapeDtypeStruct((T, D), F32),
        scratch_shapes=[pltpu.VMEM((D, tb), F32), pltpu.VMEM((eb_size, tb), BF16),
                        pltpu.VMEM((H, K, tb), BF16), pltpu.VMEM((H, K, tb), BF16)],
        compiler_params=_params("parallel", "arbitrary"),
        name="peer_dense",
    )(xn_t, u, vt, n, ea, rb, eb, h1, gf)


def _pick(total, want):
    b = min(total, want)
    while total % b:
        b //= 2
    return b


def kernel(x, norm1_g, w_in, ret_gn_g, swa_sinks, rel_bias, w_out, norm2_g,
           peer_wq, peer_subkeys, peer_u, peer_v, final_g):
    B, S, D = x.shape
    T = B * S
    depth = w_in.shape[0]
    n_exp = peer_u.shape[1]

    tm = _pick(S, 512)
    tc = _pick(S, 1024)
    tq = _pick(S, 512)
    tb_prep = _pick(T, 512)
    tb = _pick(T, 512)
    eb_size = _pick(n_exp, 1024)

    inv = 1.0 / (ROPE_BASE ** (np.arange(0, RET_DK, 2, dtype=np.float32) / RET_DK))
    ang = np.arange(S, dtype=np.float64)[:, None] * inv.astype(np.float64)[None, :]
    cos_t = jnp.asarray(np.tile(np.concatenate([np.cos(ang), np.cos(ang)], axis=1), (1, RET_HEADS)), F32)
    sin_t = jnp.asarray(np.tile(np.concatenate([-np.sin(ang), np.sin(ang)], axis=1), (1, RET_HEADS)), F32)

    bias = _swa_bias(rel_bias)
    gf = final_g.reshape(1, D).astype(F32)
    h = x.reshape(T, D)
    for l in range(depth):
        wl = w_in[l]
        v_off = 2 * RET_Q + 2 * RET_V + SWA_Q + SWA_KV
        v_cols = [wl[:, v_off + k * SWA_DH:v_off + (k + 1) * SWA_DH] for k in range(SWA_KV_HEADS)]
        w1 = jnp.concatenate([wl[:, :v_off]] + [c for c in v_cols for _ in range(2)], axis=1).astype(BF16)
        rq, rk, rv, rg, sq, sk, svv = _inproj(h, norm1_g[l].reshape(1, D), w1, cos_t, sin_t, S, tm)
        ret_out = _retention(rq, rk, rv, rg, ret_gn_g[l].reshape(1, RET_V), B, S, tc)
        swa_out = _swa(sq, sk, svv, bias, swa_sinks[l].astype(F32), B, S, tq)
        h1, xn_t = _outproj(h, ret_out, swa_out, w_out[l].astype(BF16), norm2_g[l].reshape(1, D), tm)
        wq_t = peer_wq[l].T.astype(BF16)
        skeys = peer_subkeys[l].reshape(2 * PEER_HEADS, PEER_NKEYS, PEER_DQ // 2).astype(BF16)
        n, ea, rb, eb = _peer_prep(xn_t, wq_t, skeys, tb_prep)
        u = peer_u[l].astype(BF16)
        vt = peer_v[l].T.astype(BF16)
        h = _peer_dense(xn_t, u, vt, n, ea, rb, eb, h1, gf, tb, eb_size, l == depth - 1)
    return h.reshape(B, S, D)
```

```python
import functools
import math

import numpy as np
import jax
import jax.numpy as jnp
from jax import lax
from jax.experimental import pallas as pl
from jax.experimental.pallas import tpu as pltpu

F32 = jnp.float32
BF16 = jnp.bfloat16

EPS = 1e-6
RET_HEADS = 4
RET_DK = 64
RET_DV = 128
RET_CHUNK = 128
ROPE_BASE = 10000.0
SWA_HEADS = 8
SWA_KV_HEADS = 2
SWA_DH = 64
WINDOW = 128
REL_BUCKETS = 32
REL_MAX_DIST = 128
PEER_HEADS = 8
PEER_NKEYS = 128
PEER_DQ = 256
PEER_TOPK = 16

RET_Q = RET_HEADS * RET_DK
RET_V = RET_HEADS * RET_DV
SWA_Q = SWA_HEADS * SWA_DH
SWA_KV = SWA_KV_HEADS * SWA_DH
SWA_GROUP = SWA_HEADS // SWA_KV_HEADS

LANES = 128
BF16_ROWS = 16
VMEM_LIMIT = 48 * 1024 * 1024
DENSE_ROW_CHUNK = 32
NEG = -1e30

_STAIR_COUNTS = [PEER_TOPK // (k + 1) for k in range(PEER_TOPK)]
_STAIR_OFFS = [int(v) for v in np.cumsum([0] + _STAIR_COUNTS[:-1])]
_STAIR_ROWS = int(sum(_STAIR_COUNTS))
_STAIR_PAD = -(-_STAIR_ROWS // 8) * 8


def _dot(a, b):
    return jnp.dot(a, b, preferred_element_type=F32)


def _dot_nt(a, b):
    return lax.dot_general(a, b, (((1,), (1,)), ((), ())), preferred_element_type=F32)


def _dot_tn(a, b):
    return lax.dot_general(a, b, (((0,), (0,)), ((), ())), preferred_element_type=F32)


def _params(*sem):
    return pltpu.CompilerParams(dimension_semantics=sem, vmem_limit_bytes=VMEM_LIMIT)


def _inproj_kernel(x_ref, g_ref, w_ref, cos_ref, sin_ref,
                   rq_ref, rk_ref, rv_ref, rg_ref, sq_ref, sk_ref, svv_ref):
    x = x_ref[...]
    y = x * lax.rsqrt(jnp.mean(x * x, axis=-1, keepdims=True) + EPS) * g_ref[...]
    proj = _dot(y.astype(BF16), w_ref[...])
    cos = cos_ref[...]
    sin = sin_ref[...]
    lane = lax.broadcasted_iota(jnp.int32, cos.shape, 1)
    first_half = (lane % RET_DK) < (RET_DK // 2)

    def rot(v):
        swapped = jnp.where(first_half, pltpu.roll(v, RET_Q - RET_DK // 2, 1),
                            pltpu.roll(v, RET_DK // 2, 1))
        return v * cos + swapped * sin

    c = 0
    rq = rot(proj[:, c:c + RET_Q]); c += RET_Q
    rk = rot(proj[:, c:c + RET_Q]) * (RET_DK ** -0.5); c += RET_Q
    for h in range(RET_HEADS):
        rq_ref[h] = rq[:, h * RET_DK:(h + 1) * RET_DK]
        rk_ref[h] = rk[:, h * RET_DK:(h + 1) * RET_DK]
    rv_ref[...] = proj[:, c:c + RET_V].astype(BF16); c += RET_V
    rg_ref[...] = proj[:, c:c + RET_V]; c += RET_V
    for h in range(SWA_HEADS):
        sq_ref[h] = proj[:, c + h * SWA_DH:c + (h + 1) * SWA_DH].astype(BF16)
    c += SWA_Q
    for h in range(SWA_KV_HEADS):
        sk_ref[h] = proj[:, c + h * SWA_DH:c + (h + 1) * SWA_DH].astype(BF16)
    c += SWA_KV
    svv_ref[...] = proj[:, c:c + 2 * SWA_KV].astype(BF16)


def _inproj(x2, g1, w1, cos_t, sin_t, S, tm):
    T, D = x2.shape
    n_s = S // tm
    ncol = w1.shape[1]
    tok = lambda i: (i, 0)
    hm = lambda i: (0, i, 0)
    return pl.pallas_call(
        _inproj_kernel,
        grid=(T // tm,),
        in_specs=[
            pl.BlockSpec((tm, D), tok),
            pl.BlockSpec((1, D), lambda i: (0, 0)),
            pl.BlockSpec((D, ncol), lambda i: (0, 0)),
            pl.BlockSpec((tm, RET_Q), lambda i: (i % n_s, 0)),
            pl.BlockSpec((tm, RET_Q), lambda i: (i % n_s, 0)),
        ],
        out_specs=[
            pl.BlockSpec((RET_HEADS, tm, RET_DK), hm),
            pl.BlockSpec((RET_HEADS, tm, RET_DK), hm),
            pl.BlockSpec((tm, RET_V), tok),
            pl.BlockSpec((tm, RET_V), tok),
            pl.BlockSpec((SWA_HEADS, tm, SWA_DH), hm),
            pl.BlockSpec((SWA_KV_HEADS, tm, SWA_DH), hm),
            pl.BlockSpec((tm, 2 * SWA_KV), tok),
        ],
        out_shape=[
            jax.ShapeDtypeStruct((RET_HEADS, T, RET_DK), F32),
            jax.ShapeDtypeStruct((RET_HEADS, T, RET_DK), F32),
            jax.ShapeDtypeStruct((T, RET_V), BF16),
            jax.ShapeDtypeStruct((T, RET_V), F32),
            jax.ShapeDtypeStruct((SWA_HEADS, T, SWA_DH), BF16),
            jax.ShapeDtypeStruct((SWA_KV_HEADS, T, SWA_DH), BF16),
            jax.ShapeDtypeStruct((T, 2 * SWA_KV), BF16),
        ],
        compiler_params=_params("parallel"),
        name="inproj",
    )(x2, g1, w1, cos_t, sin_t)


def _retention_kernel(q_ref, k_ref, v0_ref, v1_ref, v2_ref, v3_ref, g0_ref, g1_ref, g2_ref, g3_ref,
                      dm_ref, xi_ref, zeta_ref, cd_ref, gn_ref, o_ref, state_ref, *, n_chunks):
    v_refs = (v0_ref, v1_ref, v2_ref, v3_ref)
    g_refs = (g0_ref, g1_ref, g2_ref, g3_ref)

    @pl.when(pl.program_id(1) == 0)
    def _():
        state_ref[...] = jnp.zeros_like(state_ref)

    def chunk(c, carry):
        r = pl.multiple_of(c * RET_CHUNK, RET_CHUNK)
        for h in range(RET_HEADS):
            q = q_ref[h, pl.ds(r, RET_CHUNK), :]
            k = k_ref[h, pl.ds(r, RET_CHUNK), :]
            v = v_refs[h][pl.ds(r, RET_CHUNK), :]
            g = g_refs[h][pl.ds(r, RET_CHUNK), :]
            st = state_ref[h]
            s = _dot_nt(q.astype(BF16), k.astype(BF16)) * dm_ref[h]
            o = _dot(s.astype(BF16), v) + _dot((q * xi_ref[h]).astype(BF16), st.astype(BF16))
            state_ref[h] = st * cd_ref[h] + _dot_tn((k * zeta_ref[h]).astype(BF16), v)
            mu = jnp.mean(o, axis=-1, keepdims=True)
            oc = o - mu
            var = jnp.mean(oc * oc, axis=-1, keepdims=True)
            on = oc * lax.rsqrt(var + EPS) * gn_ref[:, h * RET_DV:(h + 1) * RET_DV]
            o_ref[h, pl.ds(r, RET_CHUNK), :] = ((g * jax.nn.sigmoid(g)) * on).astype(BF16)
        return carry

    lax.fori_loop(0, n_chunks, chunk, 0)


def _retention(rq, rk, rv, rg, gn, B, S, tc):
    T = rv.shape[0]
    n_c = S // tc
    C = RET_CHUNK
    H = RET_HEADS
    lg = np.log(1.0 - 2.0 ** (-5.0 - np.arange(H, dtype=np.float64)))
    idx = np.arange(C, dtype=np.float64)
    diff = idx[:, None] - idx[None, :]
    dm = np.where(diff[None] >= 0, np.exp(np.maximum(diff, 0.0)[None] * lg[:, None, None]), 0.0)
    xi = np.broadcast_to(np.exp((idx + 1.0)[None, :] * lg[:, None])[:, :, None], (H, C, RET_DK))
    zeta = np.broadcast_to(np.exp((C - 1.0 - idx)[None, :] * lg[:, None])[:, :, None], (H, C, RET_DK))
    cd = np.broadcast_to(np.exp(C * lg)[:, None, None], (H, RET_DK, RET_DV))
    const = lambda a: jnp.asarray(np.ascontiguousarray(a), F32)
    hm = lambda b, c: (0, b * n_c + c, 0)
    full3 = lambda b, c: (0, 0, 0)
    per_head = [pl.BlockSpec((tc, RET_DV), functools.partial(lambda b, c, h: (b * n_c + c, h), h=h))
                for h in range(H)]
    return pl.pallas_call(
        functools.partial(_retention_kernel, n_chunks=tc // C),
        grid=(B, n_c),
        in_specs=[
            pl.BlockSpec((H, tc, RET_DK), hm),
            pl.BlockSpec((H, tc, RET_DK), hm),
            *per_head,
            *per_head,
            pl.BlockSpec((H, C, C), full3),
            pl.BlockSpec((H, C, RET_DK), full3),
            pl.BlockSpec((H, C, RET_DK), full3),
            pl.BlockSpec((H, RET_DK, RET_DV), full3),
            pl.BlockSpec((1, RET_V), lambda b, c: (0, 0)),
        ],
        out_specs=pl.BlockSpec((H, tc, RET_DV), hm),
        out_shape=jax.ShapeDtypeStruct((H, T, RET_DV), BF16),
        scratch_shapes=[pltpu.VMEM((H, RET_DK, RET_DV), F32)],
        compiler_params=_params("parallel", "arbitrary"),
        name="retention",
    )(rq, rk, rv, rv, rv, rv, rg, rg, rg, rg, const(dm), const(xi), const(zeta), const(cd), gn)


def _t5_bucket(rel):
    max_exact = REL_BUCKETS // 2
    n = np.maximum(rel, 0)
    large = max_exact + (np.log(np.maximum(n, 1).astype(np.float32) / max_exact)
                         / math.log(REL_MAX_DIST / max_exact)
                         * (REL_BUCKETS - max_exact)).astype(np.int32)
    large = np.minimum(large, REL_BUCKETS - 1)
    return np.where(n < max_exact, n, large).astype(np.int32)


def _swa_bias_kernel(bucket_ref, rb_ref, o_ref):
    bucket = bucket_ref[...]
    for h in range(SWA_HEADS):
        acc = jnp.zeros(bucket.shape, F32)
        for b in range(REL_BUCKETS):
            acc = jnp.where(bucket == b, rb_ref[b, h], acc)
        o_ref[h] = acc


def _swa_bias(rel_bias):
    W = WINDOW
    rel = np.arange(W)[:, None] + W - np.arange(2 * W)[None, :]
    bucket = jnp.asarray(_t5_bucket(rel), jnp.int32)
    return pl.pallas_call(
        _swa_bias_kernel,
        in_specs=[pl.BlockSpec(memory_space=pltpu.VMEM), pl.BlockSpec(memory_space=pltpu.SMEM)],
        out_specs=pl.BlockSpec(memory_space=pltpu.VMEM),
        out_shape=jax.ShapeDtypeStruct((SWA_HEADS, W, 2 * W), F32),
        name="swa_bias",
    )(bucket, rel_bias.astype(F32))


def _swa_kernel(sink_ref, q_ref, kc_ref, kp_ref, vc_ref, vp_ref, bias_ref, o_ref, *, n_sub):
    W = WINDOW
    kvh = pl.program_id(1)
    first_step = pl.program_id(2) == 0
    qi = lax.broadcasted_iota(jnp.int32, (W, 2 * W), 0)
    kj = lax.broadcasted_iota(jnp.int32, (W, 2 * W), 1)
    rel = qi + W - kj
    band = (rel >= 0) & (rel < W)
    lane = lax.broadcasted_iota(jnp.int32, (2 * W, 2 * SWA_DH), 1)
    lo = lane < SWA_DH
    scale = SWA_DH ** -0.5
    for j in range(n_sub):
        if j == 0:
            kp = kp_ref[0]
            vp = vp_ref[...]
            valid = band & ((kj >= W) | jnp.logical_not(first_step))
        else:
            kp = kc_ref[0, (j - 1) * W:j * W, :]
            vp = vc_ref[(j - 1) * W:j * W, :]
            valid = band
        kw = jnp.concatenate([kp, kc_ref[0, j * W:(j + 1) * W, :]], axis=0)
        vw = jnp.concatenate([vp, vc_ref[j * W:(j + 1) * W, :]], axis=0)
        v_lo = jnp.where(lo, vw, jnp.zeros_like(vw))
        v_hi = jnp.where(lo, jnp.zeros_like(vw), vw)
        for pair in range(SWA_GROUP // 2):
            acc = None
            for sub, v_half in ((0, v_lo), (1, v_hi)):
                g = 2 * pair + sub
                q = q_ref[g, j * W:(j + 1) * W, :]
                s = _dot_nt(q, kw) * scale + bias_ref[g]
                s = jnp.where(valid, s, NEG)
                sink = sink_ref[kvh * SWA_GROUP + g]
                m = jnp.maximum(jnp.max(s, axis=-1, keepdims=True), sink)
                p = jnp.exp(s - m)
                den = jnp.sum(p, axis=-1, keepdims=True) + jnp.exp(sink - m)
                p = p / den
                o = _dot(p.astype(BF16), v_half)
                acc = o if acc is None else acc + o
            o_ref[j * W:(j + 1) * W, pair * 2 * SWA_DH:(pair + 1) * 2 * SWA_DH] = acc.astype(BF16)


def _swa(sq, sk, svv, bias, sinks, B, S, tq):
    T = svv.shape[0]
    W = WINDOW
    n_q = S // tq
    sub = tq // W
    cur3 = lambda b, h, i: (h, b * n_q + i, 0)
    prev3 = lambda b, h, i: (h, jnp.maximum(b * (S // W) + i * sub - 1, b * (S // W)), 0)
    cur2 = lambda b, h, i: (b * n_q + i, h)
    prev2 = lambda b, h, i: (jnp.maximum(b * (S // W) + i * sub - 1, b * (S // W)), h)
    return pl.pallas_call(
        functools.partial(_swa_kernel, n_sub=sub),
        grid=(B, SWA_KV_HEADS, n_q),
        in_specs=[
            pl.BlockSpec(memory_space=pltpu.SMEM),
            pl.BlockSpec((SWA_GROUP, tq, SWA_DH), cur3),
            pl.BlockSpec((1, tq, SWA_DH), cur3),
            pl.BlockSpec((1, W, SWA_DH), prev3),
            pl.BlockSpec((tq, 2 * SWA_DH), cur2),
            pl.BlockSpec((W, 2 * SWA_DH), prev2),
            pl.BlockSpec((SWA_GROUP, W, 2 * W), lambda b, h, i: (h, 0, 0)),
        ],
        out_specs=pl.BlockSpec((tq, SWA_GROUP * SWA_DH), cur2),
        out_shape=jax.ShapeDtypeStruct((T, SWA_Q), BF16),
        compiler_params=_params("parallel", "parallel", "arbitrary"),
        name="swa",
    )(sinks, sq, sk, sk, svv, svv, bias)


def _outproj_kernel(x_ref, ret_ref, swa_ref, w_ref, g_ref, h_ref, xn_ref):
    h = x_ref[...] + _dot(swa_ref[...], w_ref[RET_V:RET_V + SWA_Q, :])
    for hd in range(RET_HEADS):
        h = h + _dot(ret_ref[hd], w_ref[hd * RET_DV:(hd + 1) * RET_DV, :])
    h_ref[...] = h
    y = h * lax.rsqrt(jnp.mean(h * h, axis=-1, keepdims=True) + EPS) * g_ref[...]
    xn_ref[...] = y.T.astype(BF16)


def _outproj(x2, ret_out, swa_out, w_out, g2, tm):
    T, D = x2.shape
    tok = lambda i: (i, 0)
    return pl.pallas_call(
        _outproj_kernel,
        grid=(T // tm,),
        in_specs=[
            pl.BlockSpec((tm, D), tok),
            pl.BlockSpec((RET_HEADS, tm, RET_DV), lambda i: (0, i, 0)),
            pl.BlockSpec((tm, SWA_Q), tok),
            pl.BlockSpec(w_out.shape, lambda i: (0, 0)),
            pl.BlockSpec((1, D), lambda i: (0, 0)),
        ],
        out_specs=[pl.BlockSpec((tm, D), tok), pl.BlockSpec((D, tm), lambda i: (0, i))],
        out_shape=[jax.ShapeDtypeStruct((T, D), F32), jax.ShapeDtypeStruct((D, T), BF16)],
        compiler_params=_params("parallel"),
        name="outproj",
    )(x2, ret_out, swa_out, w_out, g2)


def _sort16_network():
    n, pairs, p = PEER_TOPK, [], 1
    while p < n:
        k = p
        while k >= 1:
            for j in range(k % p, n - k, 2 * k):
                for i in range(min(k, n - j - k)):
                    if (i + j) // (2 * p) == (i + j + k) // (2 * p):
                        pairs.append((i + j, i + j + k))
            k //= 2
        p *= 2
    return pairs


_SORT16 = _sort16_network()


def _top16_sorted(x3):
    s = [x3[v] for v in range(PEER_TOPK)]

    def exchange(i, j):
        s[i], s[j] = jnp.maximum(s[i], s[j]), jnp.minimum(s[i], s[j])

    for i, j in _SORT16:
        exchange(i, j)
    shift = 1
    while shift < x3.shape[1]:
        other = [pltpu.roll(t, shift, 0) for t in s]
        s = [jnp.maximum(s[k], other[PEER_TOPK - 1 - k]) for k in range(PEER_TOPK)]
        stride = PEER_TOPK // 2
        while stride >= 1:
            for i in range(PEER_TOPK):
                if i & stride == 0:
                    exchange(i, i + stride)
            stride //= 2
        shift *= 2
    return s


def _peer_prep_kernel(xn_ref, wq_ref, sk_ref, n_ref, ea_ref, rb_ref, eb_ref,
                      q_scr, bm_scr, cand_scr, *, n_lane_tiles):
    half = PEER_DQ // 2
    q_scr[...] = _dot(wq_ref[...], xn_ref[...]).astype(BF16)

    def head(h, carry):
        for lt in range(n_lane_tiles):
            cols = pl.ds(lt * LANES, LANES)
            ra = pl.multiple_of(h * PEER_DQ, PEER_DQ)
            a = _dot(sk_ref[2 * h], q_scr[pl.ds(ra, half), cols])
            b = _dot(sk_ref[2 * h + 1], q_scr[pl.ds(ra + half, half), cols])
            tiles = (PEER_NKEYS // 8, 8, LANES)
            a3 = a.reshape(tiles)
            b3 = b.reshape(tiles)
            avals = _top16_sorted(a3)
            bvals = _top16_sorted(b3)
            for r in range(PEER_TOPK):
                bm_scr[r:r + 1, :] = bvals[r][0:1]
            cand_scr[_STAIR_ROWS:_STAIR_PAD, :] = jnp.full((_STAIR_PAD - _STAIR_ROWS, LANES), -jnp.inf, F32)
            for k in range(PEER_TOPK):
                cnt = _STAIR_COUNTS[k]
                cand_scr[_STAIR_OFFS[k]:_STAIR_OFFS[k] + cnt, :] = avals[k][0:1] + bm_scr[0:cnt, :]
            cand = cand_scr[...]
            x = cand
            tau = None
            for r in range(PEER_TOPK):
                tau = jnp.max(x, axis=0, keepdims=True)
                x = jnp.where(x == tau, -jnp.inf, x)
            top = avals[0][0:1] + bvals[0][0:1]
            z = jnp.sum(jnp.where(cand >= tau, jnp.exp(cand - top), 0.0), axis=0, keepdims=True)
            n = jnp.zeros(tiles, F32)
            rb = jnp.zeros(tiles, F32)
            for r in range(PEER_TOPK):
                n = n + jnp.where(a3 + bvals[r][None] >= tau, 1.0, 0.0)
                rb = jnp.where(b3 < bvals[r][None], float(r + 1), rb)
            n_ref[h, :, cols] = n.reshape(a.shape)
            ea_ref[h, :, cols] = 0.5 * jnp.exp(a3 - avals[0][None]).reshape(a.shape)
            rb_ref[h, :, cols] = rb.reshape(a.shape).astype(BF16)
            eb_ref[h, :, cols] = (jnp.exp(b3 - bvals[0][None]) / z).reshape(a.shape).astype(BF16)
        return carry

    lax.fori_loop(0, PEER_HEADS, head, 0)


def _peer_prep(xn_t, wq_t, sk, tb):
    D, T = xn_t.shape
    H, K = PEER_HEADS, PEER_NKEYS
    blk = lambda i: (0, 0, i)
    return pl.pallas_call(
        functools.partial(_peer_prep_kernel, n_lane_tiles=tb // LANES),
        grid=(T // tb,),
        in_specs=[
            pl.BlockSpec((D, tb), lambda i: (0, i)),
            pl.BlockSpec(wq_t.shape, lambda i: (0, 0)),
            pl.BlockSpec(sk.shape, lambda i: (0, 0, 0)),
        ],
        out_specs=[pl.BlockSpec((H, K, tb), blk)] * 4,
        out_shape=[
            jax.ShapeDtypeStruct((H, K, T), F32),
            jax.ShapeDtypeStruct((H, K, T), F32),
            jax.ShapeDtypeStruct((H, K, T), BF16),
            jax.ShapeDtypeStruct((H, K, T), BF16),
        ],
        scratch_shapes=[
            pltpu.VMEM((H * PEER_DQ, tb), BF16),
            pltpu.VMEM((PEER_TOPK, LANES), F32),
            pltpu.VMEM((_STAIR_PAD, LANES), F32),
        ],
        compiler_params=_params("parallel"),
        name="peer_prep",
    )(xn_t, wq_t, sk)


def _peer_dense_kernel(xn_ref, u_ref, vt_ref, n_ref, ea_ref, rb_ref, eb_ref, h_ref, gf_ref,
                       o_ref, acc_ref, a_scr, rb_scr, eb_scr, *, n_i, apply_final):
    e = pl.program_id(1)

    @pl.when(e == 0)
    def _():
        acc_ref[...] = jnp.zeros_like(acc_ref)
        rb_scr[...] = rb_ref[...]
        eb_scr[...] = eb_ref[...]

    K = PEER_NKEYS
    tb = xn_ref.shape[1]
    hid = _dot(u_ref[...], xn_ref[...])
    for ii in range(n_i):
        n_bc = [jnp.broadcast_to(n_ref[hd, ii:ii + 1, :], (BF16_ROWS, tb)).astype(BF16)[None]
                for hd in range(PEER_HEADS)]
        ea_bc = [jnp.broadcast_to(ea_ref[hd, ii:ii + 1, :], (BF16_ROWS, tb)).astype(BF16)[None]
                 for hd in range(PEER_HEADS)]
        grp = (DENSE_ROW_CHUNK // BF16_ROWS, BF16_ROWS, tb)
        for rc in range(K // DENSE_ROW_CHUNK):
            rows = slice(rc * DENSE_ROW_CHUNK, (rc + 1) * DENSE_ROW_CHUNK)
            hh = hid[ii * K + rc * DENSE_ROW_CHUNK:ii * K + (rc + 1) * DENSE_ROW_CHUNK, :]
            act = hh + hh * lax.erf(hh * (2.0 ** -0.5))
            w = None
            for hd in range(PEER_HEADS):
                keep = jnp.where(rb_scr[hd, rows, :].reshape(grp) < n_bc[hd],
                                 eb_scr[hd, rows, :].reshape(grp), jnp.zeros((), BF16))
                w = ea_bc[hd] * keep if w is None else w + ea_bc[hd] * keep
            r0 = ii * K + rc * DENSE_ROW_CHUNK
            a_scr[r0:r0 + DENSE_ROW_CHUNK, :] = act.astype(BF16) * w.reshape(DENSE_ROW_CHUNK, tb)
    acc_ref[...] += _dot(vt_ref[...], a_scr[...])

    @pl.when(e == pl.num_programs(1) - 1)
    def _():
        h = h_ref[...] + acc_ref[...].T
        if apply_final:
            h = h * lax.rsqrt(jnp.mean(h * h, axis=-1, keepdims=True) + EPS) * gf_ref[...]
        o_ref[...] = h


def _peer_dense(xn_t, u, vt, n, ea, rb, eb, h1, gf, tb, eb_size, apply_final):
    D, T = xn_t.shape
    E = u.shape[0]
    H, K = PEER_HEADS, PEER_NKEYS
    sel = lambda t, e: (0, 0, t)
    tok = lambda t, e: (t, 0)
    return pl.pallas_call(
        functools.partial(_peer_dense_kernel, n_i=eb_size // K, apply_final=apply_final),
        grid=(T // tb, E // eb_size),
        in_specs=[
            pl.BlockSpec((D, tb), lambda t, e: (0, t)),
            pl.BlockSpec((eb_size, D), lambda t, e: (e, 0)),
            pl.BlockSpec((D, eb_size), lambda t, e: (0, e)),
            pl.BlockSpec((H, eb_size // K, tb), lambda t, e: (0, e, t)),
            pl.BlockSpec((H, eb_size // K, tb), lambda t, e: (0, e, t)),
            pl.BlockSpec((H, K, tb), sel),
            pl.BlockSpec((H, K, tb), sel),
            pl.BlockSpec((tb, D), tok),
            pl.BlockSpec((1, D), lambda t, e: (0, 0)),
        ],
        out_specs=pl.BlockSpec((tb, D), tok),
        out_shape=jax.ShapeDtypeStruct((T, D), F32),
        scratch_shapes=[pltpu.VMEM((D, tb), F32), pltpu.VMEM((eb_size, tb), BF16),
                        pltpu.VMEM((H, K, tb), BF16), pltpu.VMEM((H, K, tb), BF16)],
        compiler_params=_params("parallel", "arbitrary"),
        name="peer_dense",
    )(xn_t, u, vt, n, ea, rb, eb, h1, gf)


def _pick(total, want):
    b = min(total, want)
    while total % b:
        b //= 2
    return b


def kernel(x, norm1_g, w_in, ret_gn_g, swa_sinks, rel_bias, w_out, norm2_g,
           peer_wq, peer_subkeys, peer_u, peer_v, final_g):
    B, S, D = x.shape
    T = B * S
    depth = w_in.shape[0]
    n_exp = peer_u.shape[1]

    tm = _pick(S, 512)
    tc = _pick(S, 1024)
    tq = _pick(S, 512)
    tb_prep = _pick(T, 1024)
    tb = _pick(T, 512)
    eb_size = _pick(n_exp, 1024)

    inv = 1.0 / (ROPE_BASE ** (np.arange(0, RET_DK, 2, dtype=np.float32) / RET_DK))
    ang = np.arange(S, dtype=np.float64)[:, None] * inv.astype(np.float64)[None, :]
    cos_t = jnp.asarray(np.tile(np.concatenate([np.cos(ang), np.cos(ang)], axis=1), (1, RET_HEADS)), F32)
    sin_t = jnp.asarray(np.tile(np.concatenate([-np.sin(ang), np.sin(ang)], axis=1), (1, RET_HEADS)), F32)

    bias = _swa_bias(rel_bias)
    gf = final_g.reshape(1, D).astype(F32)
    h = x.reshape(T, D)
    for l in range(depth):
        wl = w_in[l]
        v_off = 2 * RET_Q + 2 * RET_V + SWA_Q + SWA_KV
        v_cols = [wl[:, v_off + k * SWA_DH:v_off + (k + 1) * SWA_DH] for k in range(SWA_KV_HEADS)]
        w1 = jnp.concatenate([wl[:, :v_off]] + [c for c in v_cols for _ in range(2)], axis=1).astype(BF16)
        rq, rk, rv, rg, sq, sk, svv = _inproj(h, norm1_g[l].reshape(1, D), w1, cos_t, sin_t, S, tm)
        ret_out = _retention(rq, rk, rv, rg, ret_gn_g[l].reshape(1, RET_V), B, S, tc)
        swa_out = _swa(sq, sk, svv, bias, swa_sinks[l].astype(F32), B, S, tq)
        h1, xn_t = _outproj(h, ret_out, swa_out, w_out[l].astype(BF16), norm2_g[l].reshape(1, D), tm)
        wq_t = peer_wq[l].T.astype(BF16)
        skeys = peer_subkeys[l].reshape(2 * PEER_HEADS, PEER_NKEYS, PEER_DQ // 2).astype(BF16)
        n, ea, rb, eb = _peer_prep(xn_t, wq_t, skeys, tb_prep)
        u = peer_u[l].astype(BF16)
        vt = peer_v[l].T.astype(BF16)
        h = _peer_dense(xn_t, u, vt, n, ea, rb, eb, h1, gf, tb, eb_size, l == depth - 1)
    return h.reshape(B, S, D)
```
